```python
import math, functools
import jax, jax.numpy as jnp
from jax import lax
import numpy as np

D_MODEL = 4096
BATCH = 4
SEQ = 2048
DEPTH = 2
DEC_BATCH = 8
DEC_SEQ = 1
PAST_LEN = 16384
PAGE_SIZE = 128

N_A_LAYERS = DEPTH // 2
N_B_LAYERS = DEPTH - N_A_LAYERS
SSM_GROUP = 16
N_GROUPS = D_MODEL // SSM_GROUP
SSM_STATE = 64
DT_MIN = 1e-3
DT_MAX = 1e-1
N_HEADS = 32
HEAD_DIM = D_MODEL // N_HEADS
D_FF = -(-8 * D_MODEL // (3 * 256)) * 256
Q_BLOCK = 128
RMS_EPS = 1e-6
FORGET_BIAS_INIT = 3.0
CACHE_FORGET_LOGIT = 10.0

kernel_name = 'yoco_s5_fox_decoder_step'


def rmsnorm(x, g):
    xf = x.astype(jnp.float32)
    y = xf * lax.rsqrt(jnp.mean(xf * xf, axis=-1, keepdims=True) + RMS_EPS)
    return (y * g.astype(jnp.float32)).astype(x.dtype)


def swiglu(h, w_gate_up, w_down):
    g, u = jnp.split(h @ w_gate_up, 2, axis=-1)
    return (jax.nn.silu(g) * u) @ w_down


def _cmul(ar, ai, br, bi):
    return ar * br - ai * bi, ar * bi + ai * br


def _scan_combine(e1, e2):
    a1r, a1i, b1r, b1i = e1
    a2r, a2i, b2r, b2i = e2
    ar, ai = _cmul(a2r, a2i, a1r, a1i)
    br, bi = _cmul(a2r, a2i, b1r, b1i)
    return ar, ai, br + b2r, bi + b2i


def s5_mixer(h, x0_re, x0_im, w_in, lam_re, lam_im, log_dt, b_re, b_im, c_re, c_im, d_skip, w_glu):
    f32 = jnp.float32
    bsz, length, _ = h.shape
    u = (h @ w_in).astype(f32).reshape(bsz, length, N_GROUPS, SSM_GROUP)
    lam_re = lam_re.astype(f32)
    lam_im = lam_im.astype(f32)
    dt = jnp.exp(log_dt.astype(f32))[:, None]
    mag = jnp.exp(lam_re * dt)
    ab_re = mag * jnp.cos(lam_im * dt)
    ab_im = mag * jnp.sin(lam_im * dt)
    den = lam_re * lam_re + lam_im * lam_im
    co_re = ((ab_re - 1.0) * lam_re + ab_im * lam_im) / den
    co_im = (ab_im * lam_re - (ab_re - 1.0) * lam_im) / den
    b_re = b_re.astype(f32)
    b_im = b_im.astype(f32)
    bb_re = co_re[..., None] * b_re - co_im[..., None] * b_im
    bb_im = co_re[..., None] * b_im + co_im[..., None] * b_re
    bu_re = jnp.einsum('blgc,gpc->blgp', u, bb_re)
    bu_im = jnp.einsum('blgc,gpc->blgp', u, bb_im)
    s0_re, s0_im = _cmul(ab_re, ab_im, x0_re.astype(f32), x0_im.astype(f32))
    bu_re = bu_re.at[:, 0].add(s0_re)
    bu_im = bu_im.at[:, 0].add(s0_im)

    def scan_one(bu):
        br, bi = bu
        ar = jnp.broadcast_to(ab_re, br.shape)
        ai = jnp.broadcast_to(ab_im, bi.shape)
        _, _, sr, si = lax.associative_scan(_scan_combine, (ar, ai, br, bi), axis=0)
        return sr, si

    s_re, s_im = lax.map(scan_one, (bu_re, bu_im))
    y = (jnp.einsum('blgp,gcp->blgc', s_re, c_re.astype(f32))
         - jnp.einsum('blgp,gcp->blgc', s_im, c_im.astype(f32))
         + d_skip.astype(f32).reshape(N_GROUPS, SSM_GROUP) * u)
    g = jax.nn.gelu(y.reshape(bsz, length, D_MODEL)).astype(h.dtype)
    za, zb = jnp.split(g @ w_glu, 2, axis=-1)
    return za * jax.nn.sigmoid(zb), s_re[:, -1], s_im[:, -1]


def fox_prompt(q, k, v, logf):
    bsz, length = q.shape[:2]
    n_blk = length // Q_BLOCK
    scale = HEAD_DIM ** -0.5
    c_k = lax.cumsum(logf.astype(jnp.float32), axis=1).transpose(0, 2, 1)
    q_blocks = q.reshape(bsz, n_blk, Q_BLOCK, N_HEADS, HEAD_DIM).transpose(1, 0, 2, 3, 4)
    c_blocks = c_k.reshape(bsz, N_HEADS, n_blk, Q_BLOCK).transpose(2, 0, 1, 3)
    key_pos = jnp.arange(length)

    def one_block(args):
        blk, q_blk, c_q = args
        s = (jnp.einsum('bqhd,bkhd->bhqk', q_blk, k).astype(jnp.float32) * scale
             + c_q[..., None] - c_k[:, :, None, :])
        q_pos = blk * Q_BLOCK + jnp.arange(Q_BLOCK)
        s = jnp.where(key_pos[None, :] <= q_pos[:, None], s, -jnp.inf)
        p = jax.nn.softmax(s, axis=-1).astype(v.dtype)
        return jnp.einsum('bhqk,bkhd->bqhd', p, v)

    out = lax.map(one_block, (jnp.arange(n_blk), q_blocks, c_blocks))
    return out.transpose(1, 0, 2, 3, 4).reshape(bsz, length, D_MODEL)


def fox_sample(q, k_new, v_new, logf_new, k_past, v_past, logf_past):
    bsz, t_len = q.shape[:2]
    p_len = k_past.shape[1]
    scale = HEAD_DIM ** -0.5
    lp = logf_past.astype(jnp.float32)
    suffix = (lax.cumsum(lp, axis=1, reverse=True) - lp).transpose(0, 2, 1)
    cn = lax.cumsum(logf_new.astype(jnp.float32), axis=1).transpose(0, 2, 1)
    s_past = (jnp.einsum('bqhd,bkhd->bhqk', q, k_past).astype(jnp.float32) * scale
              + cn[..., :, None] + suffix[:, :, None, :])
    s_new = (jnp.einsum('bqhd,bkhd->bhqk', q, k_new).astype(jnp.float32) * scale
             + cn[..., :, None] - cn[..., None, :])
    causal = jnp.arange(t_len)[None, :] <= jnp.arange(t_len)[:, None]
    s_new = jnp.where(causal, s_new, -jnp.inf)
    p = jax.nn.softmax(jnp.concatenate([s_past, s_new], axis=-1), axis=-1).astype(v_new.dtype)
    out = (jnp.einsum('bhqk,bkhd->bqhd', p[..., :p_len], v_past)
           + jnp.einsum('bhqk,bkhd->bqhd', p[..., p_len:], v_new))
    return out.reshape(bsz, t_len, D_MODEL)


def run_trunk(x, x0_re, x0_im, k_past, v_past, logf_past, norm_mix, norm_ffn, ssm_w_in, ssm_lambda_re,
              ssm_lambda_im, ssm_log_dt, ssm_b_re, ssm_b_im, ssm_c_re, ssm_c_im, ssm_d, ssm_w_glu,
              attn_w_q, attn_w_o, norm_kv, kv_w_k, kv_w_v, kv_w_f, kv_b_f, ffn_w_gate_up, ffn_w_down,
              norm_final):
    bsz, length, _ = x.shape
    h = x
    fin_re, fin_im = [], []
    k = v = logf = None
    for layer in range(DEPTH):
        if layer < N_A_LAYERS:
            a = layer
            mix, sr, si = s5_mixer(rmsnorm(h, norm_mix[layer]), x0_re[a], x0_im[a], ssm_w_in[a],
                                   ssm_lambda_re[a], ssm_lambda_im[a], ssm_log_dt[a], ssm_b_re[a],
                                   ssm_b_im[a], ssm_c_re[a], ssm_c_im[a], ssm_d[a], ssm_w_glu[a])
            fin_re.append(sr)
            fin_im.append(si)
        else:
            if layer == N_A_LAYERS:
                z = rmsnorm(h, norm_kv)
                k = (z @ kv_w_k).reshape(bsz, length, N_HEADS, HEAD_DIM)
                v = (z @ kv_w_v).reshape(bsz, length, N_HEADS, HEAD_DIM)
                logf = jax.nn.log_sigmoid((z @ kv_w_f + kv_b_f).astype(jnp.float32))
            bl = layer - N_A_LAYERS
            q = (rmsnorm(h, norm_mix[layer]) @ attn_w_q[bl]).reshape(bsz, length, N_HEADS, HEAD_DIM)
            if k_past is None:
                att = fox_prompt(q, k, v, logf)
            else:
                att = fox_sample(q, k, v, logf, k_past, v_past, logf_past)
            mix = att @ attn_w_o[bl]
        h = h + mix
        h = h + swiglu(rmsnorm(h, norm_ffn[layer]), ffn_w_gate_up[layer], ffn_w_down[layer])
    y = rmsnorm(h, norm_final)
    return y, jnp.stack(fin_re), jnp.stack(fin_im), k, v, logf


def setup_inputs(seed: int = 0) -> dict:
    key = jax.random.key(seed)
    ks = jax.random.split(key, 32)
    f32 = jnp.float32
    n_pages = PAST_LEN // PAGE_SIZE
    n_used = DEC_BATCH * n_pages
    n_pool = (n_used * 5) // 4

    def nrm(k, shape, scale):
        return jax.random.normal(k, shape, f32) * scale

    ssm_shape = (N_A_LAYERS, N_GROUPS, SSM_STATE)
    n_idx = jnp.arange(SSM_STATE, dtype=f32)
    page_table = jax.random.permutation(ks[7], n_pool)[:n_used].reshape(DEC_BATCH, n_pages).astype(jnp.int32)
    return {
        'x_prompt': nrm(ks[0], (BATCH, SEQ, D_MODEL), 1.0),
        'x_sample': nrm(ks[1], (DEC_BATCH, DEC_SEQ, D_MODEL), 1.0),
        'state_ssm_re': nrm(ks[2], (N_A_LAYERS, DEC_BATCH, N_GROUPS, SSM_STATE), 0.3),
        'state_ssm_im': nrm(ks[3], (N_A_LAYERS, DEC_BATCH, N_GROUPS, SSM_STATE), 0.3),
        'cache_k': nrm(ks[4], (n_pool, PAGE_SIZE, N_HEADS, HEAD_DIM), 1.0),
        'cache_v': nrm(ks[5], (n_pool, PAGE_SIZE, N_HEADS, HEAD_DIM), 1.0),
        'cache_logf': jax.nn.log_sigmoid(CACHE_FORGET_LOGIT + nrm(ks[6], (n_pool, PAGE_SIZE, N_HEADS), 1.0)),
        'page_table': page_table,
        'norm_mix': 1.0 + nrm(ks[8], (DEPTH, D_MODEL), 0.02),
        'norm_ffn': 1.0 + nrm(ks[9], (DEPTH, D_MODEL), 0.02),
        'ssm_w_in': nrm(ks[10], (N_A_LAYERS, D_MODEL, D_MODEL), D_MODEL ** -0.5),
        'ssm_lambda_re': -0.5 + nrm(ks[11], ssm_shape, 0.01),
        'ssm_lambda_im': math.pi * n_idx + nrm(ks[12], ssm_shape, 0.01),
        'ssm_log_dt': jax.random.uniform(ks[13], (N_A_LAYERS, N_GROUPS), f32, math.log(DT_MIN), math.log(DT_MAX)),
        'ssm_b_re': nrm(ks[14], (N_A_LAYERS, N_GROUPS, SSM_STATE, SSM_GROUP), (2 * SSM_GROUP) ** -0.5),
        'ssm_b_im': nrm(ks[15], (N_A_LAYERS, N_GROUPS, SSM_STATE, SSM_GROUP), (2 * SSM_GROUP) ** -0.5),
        'ssm_c_re': nrm(ks[16], (N_A_LAYERS, N_GROUPS, SSM_GROUP, SSM_STATE), SSM_STATE ** -0.5),
        'ssm_c_im': nrm(ks[17], (N_A_LAYERS, N_GROUPS, SSM_GROUP, SSM_STATE), SSM_STATE ** -0.5),
        'ssm_d': nrm(ks[18], (N_A_LAYERS, D_MODEL), 1.0),
        'ssm_w_glu': nrm(ks[19], (N_A_LAYERS, D_MODEL, 2 * D_MODEL), D_MODEL ** -0.5),
        'attn_w_q': nrm(ks[20], (N_B_LAYERS, D_MODEL, D_MODEL), D_MODEL ** -0.5),
        'attn_w_o': nrm(ks[21], (N_B_LAYERS, D_MODEL, D_MODEL), D_MODEL ** -0.5),
        'norm_kv': 1.0 + nrm(ks[22], (D_MODEL,), 0.02),
        'kv_w_k': nrm(ks[23], (D_MODEL, D_MODEL), D_MODEL ** -0.5),
        'kv_w_v': nrm(ks[24], (D_MODEL, D_MODEL), D_MODEL ** -0.5),
        'kv_w_f': nrm(ks[25], (D_MODEL, N_HEADS), D_MODEL ** -0.5),
        'kv_b_f': FORGET_BIAS_INIT + nrm(ks[26], (N_HEADS,), 0.1),
        'ffn_w_gate_up': nrm(ks[27], (DEPTH, D_MODEL, 2 * D_FF), D_MODEL ** -0.5),
        'ffn_w_down': nrm(ks[28], (DEPTH, D_FF, D_MODEL), D_FF ** -0.5),
        'norm_final': 1.0 + nrm(ks[29], (D_MODEL,), 0.02),
    }


def reference(x_prompt, x_sample, state_ssm_re, state_ssm_im, cache_k, cache_v, cache_logf, page_table,
              norm_mix, norm_ffn, ssm_w_in, ssm_lambda_re, ssm_lambda_im, ssm_log_dt, ssm_b_re, ssm_b_im,
              ssm_c_re, ssm_c_im, ssm_d, ssm_w_glu, attn_w_q, attn_w_o, norm_kv, kv_w_k, kv_w_v, kv_w_f,
              kv_b_f, ffn_w_gate_up, ffn_w_down, norm_final):
    trunk = functools.partial(
        run_trunk, norm_mix=norm_mix, norm_ffn=norm_ffn, ssm_w_in=ssm_w_in, ssm_lambda_re=ssm_lambda_re,
        ssm_lambda_im=ssm_lambda_im, ssm_log_dt=ssm_log_dt, ssm_b_re=ssm_b_re, ssm_b_im=ssm_b_im,
        ssm_c_re=ssm_c_re, ssm_c_im=ssm_c_im, ssm_d=ssm_d, ssm_w_glu=ssm_w_glu, attn_w_q=attn_w_q,
        attn_w_o=attn_w_o, norm_kv=norm_kv, kv_w_k=kv_w_k, kv_w_v=kv_w_v, kv_w_f=kv_w_f, kv_b_f=kv_b_f,
        ffn_w_gate_up=ffn_w_gate_up, ffn_w_down=ffn_w_down, norm_final=norm_final)

    zeros_state = jnp.zeros((N_A_LAYERS, x_prompt.shape[0], N_GROUPS, SSM_STATE), jnp.float32)
    y_prompt, ssm_re_prompt, ssm_im_prompt, k_prompt, v_prompt, logf_prompt = trunk(
        x_prompt, zeros_state, zeros_state, None, None, None)

    dec_batch = page_table.shape[0]
    k_past = cache_k[page_table].reshape(dec_batch, -1, N_HEADS, HEAD_DIM)
    v_past = cache_v[page_table].reshape(dec_batch, -1, N_HEADS, HEAD_DIM)
    logf_past = cache_logf[page_table].reshape(dec_batch, -1, N_HEADS)
    y_sample, ssm_re_sample, ssm_im_sample, k_sample, v_sample, logf_sample = trunk(
        x_sample, state_ssm_re, state_ssm_im, k_past, v_past, logf_past)

    return (y_prompt, y_sample, ssm_re_prompt, ssm_im_prompt, k_prompt, v_prompt, logf_prompt,
            ssm_re_sample, ssm_im_sample, k_sample, v_sample, logf_sample)
```

```python
import functools
import math

import jax
import jax.numpy as jnp
from jax import lax
from jax.experimental import pallas as pl
from jax.experimental.pallas import tpu as pltpu

F32 = jnp.float32
BF16 = jnp.bfloat16
RMS_EPS = 1e-6
SSM_CHUNK = 16
V7X_VMEM_LIMIT = 56 * 1024 * 1024
HIGHEST = lax.Precision.HIGHEST


def _params(semantics, vmem_bytes=V7X_VMEM_LIMIT):
    return pltpu.CompilerParams(dimension_semantics=semantics, vmem_limit_bytes=vmem_bytes)


def _sigmoid(x):
    return 1.0 / (1.0 + jnp.exp(-x))


def _rmsnorm_body(x_ref, g_ref, o_ref):
    x = x_ref[...]
    y = x * lax.rsqrt(jnp.mean(x * x, axis=-1, keepdims=True) + RMS_EPS)
    o_ref[...] = (y * g_ref[...]).astype(o_ref.dtype)


def _rmsnorm(x, g, out_dtype, bm):
    m, d = x.shape
    bm = min(bm, m)
    return pl.pallas_call(
        _rmsnorm_body,
        grid=(m // bm,),
        in_specs=[pl.BlockSpec((bm, d), lambda i: (i, 0)), pl.BlockSpec((1, d), lambda i: (0, 0))],
        out_specs=pl.BlockSpec((bm, d), lambda i: (i, 0)),
        out_shape=jax.ShapeDtypeStruct((m, d), out_dtype),
        compiler_params=_params(("parallel",)),
        name="rmsnorm",
    )(x, g.reshape(1, d))


def _mm_body(nw, n_tile, n_row, n_out, nk, epi, *refs):
    x_ref = refs[0]
    w_refs = refs[1:1 + nw]
    t_refs = refs[1 + nw:1 + nw + n_tile]
    r_refs = refs[1 + nw + n_tile:1 + nw + n_tile + n_row]
    o_refs = refs[1 + nw + n_tile + n_row:1 + nw + n_tile + n_row + n_out]
    acc_refs = refs[1 + nw + n_tile + n_row + n_out:]
    x = x_ref[...].astype(BF16)
    parts = [jnp.dot(x, w[...], preferred_element_type=F32) for w in w_refs]

    def finish(accs):
        outs = epi(accs, [t[...] for t in t_refs], [r[...] for r in r_refs])
        for o, v in zip(o_refs, outs):
            o[...] = v.astype(o.dtype)

    if nk == 1:
        finish(parts)
    else:
        k = pl.program_id(2)

        @pl.when(k == 0)
        def _():
            for a, p in zip(acc_refs, parts):
                a[...] = p

        @pl.when(jnp.logical_and(k > 0, k < nk - 1))
        def _():
            for a, p in zip(acc_refs, parts):
                a[...] += p

        @pl.when(k == nk - 1)
        def _():
            finish([a[...] + p for a, p in zip(acc_refs, parts)])


def _matmul(x, w, n_cols, col_offsets, epi, out_dtypes, *, bm, bn, bk=None, tiles=(), rows=()):
    m, kdim = x.shape
    bm = min(bm, m)
    bn = min(bn, n_cols)
    bk = kdim if bk is None else bk
    assert m % bm == 0 and n_cols % bn == 0 and kdim % bk == 0, (x.shape, w.shape, bm, bn, bk)
    assert all(off % bn == 0 for off in col_offsets)
    nk = kdim // bk
    nw = len(col_offsets)
    in_specs = [pl.BlockSpec((bm, bk), lambda i, j, k: (i, k))]
    for off in col_offsets:
        in_specs.append(pl.BlockSpec((bk, bn), functools.partial(lambda i, j, k, o: (k, j + o), o=off // bn)))
    in_specs += [pl.BlockSpec((bm, bn), lambda i, j, k: (i, j)) for _ in tiles]
    in_specs += [pl.BlockSpec((1, bn), lambda i, j, k: (0, j)) for _ in rows]
    body = functools.partial(_mm_body, nw, len(tiles), len(rows), len(out_dtypes), nk, epi)
    outs = pl.pallas_call(
        body,
        grid=(m // bm, n_cols // bn, nk),
        in_specs=in_specs,
        out_specs=[pl.BlockSpec((bm, bn), lambda i, j, k: (i, j)) for _ in out_dtypes],
        out_shape=[jax.ShapeDtypeStruct((m, n_cols), dt) for dt in out_dtypes],
        scratch_shapes=[pltpu.VMEM((bm, bn), F32) for _ in range(nw if nk > 1 else 0)],
        compiler_params=_params(("parallel", "parallel", "arbitrary")),
        name="matmul_" + epi.__name__.strip("_"),
    )(x, *([w] * nw), *tiles, *rows)
    return outs


def _epi_plain(accs, tiles, rows):
    return (accs[0],)


def _epi_dup(accs, tiles, rows):
    return (accs[0], accs[0])


def _epi_residual(accs, tiles, rows):
    return (tiles[0] + accs[0],)


def _epi_swiglu(accs, tiles, rows):
    g, u = accs
    return (g * _sigmoid(g) * u,)


def _epi_glu_residual(accs, tiles, rows):
    za, zb = accs
    return (tiles[0] + za * _sigmoid(zb),)


def _epi_log_sigmoid(accs, tiles, rows):
    z = accs[0] + rows[0]
    return (jnp.minimum(z, 0.0) - jnp.log1p(jnp.exp(-jnp.abs(z))),)


def _cmul(ar, ai, br, bi):
    return ar * br - ai * bi, ar * bi + ai * br


def _s5_prep_body(n_sq, lre_ref, lim_ref, ldt_ref, btr_ref, bti_ref, cre_ref, cim_ref,
                  ab_ref, bb_ref, kt_ref, wsr_ref, wsi_ref, wyr_ref, wyi_ref, pw_ref):
    lam_re = lre_ref[...]
    lam_im = lim_ref[...]
    dt = jnp.exp(ldt_ref[...])
    mag = jnp.exp(lam_re * dt)
    ab_re = mag * jnp.cos(lam_im * dt)
    ab_im = mag * jnp.sin(lam_im * dt)
    den = lam_re * lam_re + lam_im * lam_im
    co_re = ((ab_re - 1.0) * lam_re + ab_im * lam_im) / den
    co_im = (ab_im * lam_re - (ab_re - 1.0) * lam_im) / den
    bt_re = btr_ref[...]
    bt_im = bti_ref[...]
    bb_re = co_re[:, None, :] * bt_re - co_im[:, None, :] * bt_im
    bb_im = co_re[:, None, :] * bt_im + co_im[:, None, :] * bt_re
    c_re = cre_ref[...]
    c_im = cim_ref[...]
    ab_ref[0] = ab_re
    ab_ref[1] = ab_im
    bb_ref[0] = bb_re
    bb_ref[1] = bb_im

    pw = [(jnp.ones_like(ab_re), jnp.zeros_like(ab_im))]
    for _ in range(SSM_CHUNK):
        pw.append(_cmul(pw[-1][0], pw[-1][1], ab_re, ab_im))

    nt = (((2,), (2,)), ((0,), (0,)))
    for d in range(SSM_CHUNK):
        pr, pi = pw[d]
        m_re, m_im = _cmul(c_re, c_im, pr[:, None, :], pi[:, None, :])
        kt_ref[:, d] = (lax.dot_general(m_re, bb_re, nt, precision=HIGHEST, preferred_element_type=F32)
                        - lax.dot_general(m_im, bb_im, nt, precision=HIGHEST, preferred_element_type=F32))
        wr, wi = _cmul(bb_re, bb_im, pr[:, None, :], pi[:, None, :])
        wsr_ref[:, SSM_CHUNK - 1 - d] = wr
        wsi_ref[:, SSM_CHUNK - 1 - d] = wi
        qr, qi = pw[d + 1]
        yr, yi = _cmul(c_re, c_im, qr[:, None, :], qi[:, None, :])
        wyr_ref[:, d] = yr
        wyi_ref[:, d] = -yi

    qr, qi = pw[SSM_CHUNK]
    for k in range(n_sq):
        pw_ref[0, k] = qr
        pw_ref[1, k] = qi
        qr, qi = _cmul(qr, qi, qr, qi)


def _s5_prep(lam_re, lam_im, log_dt, b_re, b_im, c_re, c_im, n_sq, gt=8):
    g, p = lam_re.shape
    c = c_re.shape[1]
    gt = min(gt, g)
    t = SSM_CHUNK
    bt_re = jnp.swapaxes(b_re, 1, 2)
    bt_im = jnp.swapaxes(b_im, 1, 2)
    gp = pl.BlockSpec((gt, p), lambda i: (i, 0))
    gcp = pl.BlockSpec((gt, c, p), lambda i: (i, 0, 0))
    gtcp = pl.BlockSpec((gt, t, c, p), lambda i: (i, 0, 0, 0))
    out_shapes = [
        jax.ShapeDtypeStruct((2, g, p), F32),
        jax.ShapeDtypeStruct((2, g, c, p), F32),
        jax.ShapeDtypeStruct((g, t, c, c), F32),
        jax.ShapeDtypeStruct((g, t, c, p), F32),
        jax.ShapeDtypeStruct((g, t, c, p), F32),
        jax.ShapeDtypeStruct((g, t, c, p), F32),
        jax.ShapeDtypeStruct((g, t, c, p), F32),
        jax.ShapeDtypeStruct((2, n_sq, g, p), F32),
    ]
    out_specs = [
        pl.BlockSpec((2, gt, p), lambda i: (0, i, 0)),
        pl.BlockSpec((2, gt, c, p), lambda i: (0, i, 0, 0)),
        pl.BlockSpec((gt, t, c, c), lambda i: (i, 0, 0, 0)),
        gtcp, gtcp, gtcp, gtcp,
        pl.BlockSpec((2, n_sq, gt, p), lambda i: (0, 0, i, 0)),
    ]
    return pl.pallas_call(
        functools.partial(_s5_prep_body, n_sq),
        grid=(g // gt,),
        in_specs=[gp, gp, pl.BlockSpec((gt, 1), lambda i: (i, 0)), gcp, gcp, gcp, gcp],
        out_specs=out_specs,
        out_shape=out_shapes,
        compiler_params=_params(("parallel",)),
        name="s5_prep",
    )(lam_re, lam_im, log_dt.reshape(g, 1), bt_re, bt_im, c_re, c_im)


def _s5_chunk_weights(kt, ws_re, ws_im, wy_re, wy_im):
    g, t, c, p = ws_re.shape
    j = jnp.arange(t)[:, None]
    tau = jnp.arange(t)[None, :]
    lag = tau - j
    blocks = kt[:, jnp.clip(lag, 0, t - 1)]
    blocks = jnp.where((lag >= 0)[None, :, :, None, None], blocks, 0.0)
    tz = blocks.transpose(0, 1, 4, 2, 3).reshape(g, t * c, t * c)
    ws = jnp.concatenate([ws_re, ws_im], axis=-1).reshape(g, t * c, 2 * p)
    wy = jnp.concatenate([wy_re, wy_im], axis=-1)
    wy = wy.transpose(0, 3, 1, 2).reshape(g, 2 * p, t * c)
    return tz.astype(BF16), ws.astype(BF16), wy.astype(BF16)


def _s5_scan_body(gt, nb, nc, n_sq, p, x_ref, tz_ref, ws_ref, wy_ref, pw_ref, x0_ref, y_ref, fin_ref):
    row = lax.broadcasted_iota(jnp.int32, (nc, 2 * p), 0)
    lane = lax.broadcasted_iota(jnp.int32, (nc, 2 * p), 1)
    sign = jnp.where(lane < p, -1.0, 1.0)

    def cmul_packed(ar, ai, z):
        return ar * z + (ai * sign) * pltpu.roll(z, p, 1)

    for gi in range(gt):
        x = x_ref[gi]
        y_intra = jnp.dot(x, tz_ref[gi], preferred_element_type=F32)
        e = jnp.dot(x, ws_ref[gi], preferred_element_type=F32)
        ar = [pw_ref[gi, k:k + 1, :] for k in range(n_sq)]
        ai = [pw_ref[gi, n_sq + k:n_sq + k + 1, :] for k in range(n_sq)]
        s_prev = []
        for b in range(nb):
            z = e[b * nc:(b + 1) * nc]
            x0 = x0_ref[gi, b:b + 1, :]
            z = z + jnp.where(row == 0, cmul_packed(ar[0], ai[0], jnp.broadcast_to(x0, (nc, 2 * p))), 0.0)
            for k in range(n_sq):
                sh = 1 << k
                zs = jnp.where(row >= sh, pltpu.roll(z, sh, 0), 0.0)
                z = z + cmul_packed(ar[k], ai[k], zs)
            fin_ref[gi, b:b + 1, :] = z[nc - 1:nc, :]
            s_prev.append(jnp.where(row == 0, x0, pltpu.roll(z, 1, 0)))
        s_in = jnp.concatenate(s_prev, axis=0) if nb > 1 else s_prev[0]
        y_ref[gi] = y_intra + jnp.dot(s_in.astype(BF16), wy_ref[gi], preferred_element_type=F32)


def _s5_scan(xg, tz, ws, wy, pw, x0, nb, gt=4):
    g, r, tc = xg.shape
    nc = r // nb
    n_sq = pw.shape[1] // 2
    assert (1 << n_sq) >= nc and nc % 8 == 0
    p2 = ws.shape[-1]
    gt = min(gt, g)
    body = functools.partial(_s5_scan_body, gt, nb, nc, n_sq, p2 // 2)
    return pl.pallas_call(
        body,
        grid=(g // gt,),
        in_specs=[
            pl.BlockSpec((gt, r, tc), lambda i: (i, 0, 0)),
            pl.BlockSpec((gt, tc, tc), lambda i: (i, 0, 0)),
            pl.BlockSpec((gt, tc, p2), lambda i: (i, 0, 0)),
            pl.BlockSpec((gt, p2, tc), lambda i: (i, 0, 0)),
            pl.BlockSpec((gt, 2 * n_sq, p2), lambda i: (i, 0, 0)),
            pl.BlockSpec((gt, nb, p2), lambda i: (i, 0, 0)),
        ],
        out_specs=[
            pl.BlockSpec((gt, r, tc), lambda i: (i, 0, 0)),
            pl.BlockSpec((gt, nb, p2), lambda i: (i, 0, 0)),
        ],
        out_shape=[jax.ShapeDtypeStruct((g, r, tc), F32), jax.ShapeDtypeStruct((g, nb, p2), F32)],
        compiler_params=_params(("parallel",)),
        name="s5_scan",
    )(xg, tz, ws, wy, pw, x0)


def _gelu_tanh(y):
    return 0.5 * y * (1.0 + jnp.tanh(math.sqrt(2.0 / math.pi) * (y + 0.044715 * (y * y * y))))


def _s5_post_body(y_ref, u_ref, d_ref, o_ref):
    o_ref[...] = _gelu_tanh(y_ref[...] + d_ref[...] * u_ref[...]).astype(o_ref.dtype)


def _s5_post(y, u, d, out_dtype, bm):
    m, dm = y.shape
    bm = min(bm, m)
    blk = pl.BlockSpec((bm, dm), lambda i: (i, 0))
    return pl.pallas_call(
        _s5_post_body,
        grid=(m // bm,),
        in_specs=[blk, blk, pl.BlockSpec((1, dm), lambda i: (0, 0))],
        out_specs=blk,
        out_shape=jax.ShapeDtypeStruct((m, dm), out_dtype),
        compiler_params=_params(("parallel",)),
        name="s5_post",
    )(y, u, d.reshape(1, dm))


def _s5_step_body(u_ref, x0r_ref, x0i_ref, abr_ref, abi_ref, wbr_ref, wbi_ref, wcr_ref, wci_ref, d_ref,
                  g_ref, sr_ref, si_ref):
    u = u_ref[...]
    ub = u.astype(BF16)
    bu_re = jnp.dot(ub, wbr_ref[0], preferred_element_type=F32)
    bu_im = jnp.dot(ub, wbi_ref[0], preferred_element_type=F32)
    ar, ai = abr_ref[...], abi_ref[...]
    s0r, s0i = _cmul(ar, ai, x0r_ref[...], x0i_ref[...])
    s_re = bu_re + s0r
    s_im = bu_im + s0i
    sr_ref[...] = s_re
    si_ref[...] = s_im
    y = (jnp.dot(s_re.astype(BF16), wcr_ref[0], preferred_element_type=F32)
         - jnp.dot(s_im.astype(BF16), wci_ref[0], preferred_element_type=F32)
         + d_ref[...] * u)
    g_ref[...] = _gelu_tanh(y)


def _s5_step(u, x0_re, x0_im, ab, bb, c_re, c_im, d):
    bd, dm = u.shape
    g, p = ab.shape[1:]
    c = dm // g
    gl = 128 // c
    nt = g // gl
    sl = gl * p
    eye = jnp.eye(gl, dtype=F32)

    def bdiag_in(w):
        w = w.reshape(nt, gl, c, p)
        return jnp.einsum("nacp,ab->nacbp", w, eye).reshape(nt, gl * c, gl * p).astype(BF16)

    def bdiag_out(w):
        w = w.reshape(nt, gl, c, p)
        return jnp.einsum("nacp,ab->napbc", w, eye).reshape(nt, gl * p, gl * c).astype(BF16)

    row = lambda a: a.reshape(1, g * p)
    lane_blk = pl.BlockSpec((bd, 128), lambda n: (0, n))
    st_blk = pl.BlockSpec((bd, sl), lambda n: (0, n))
    st_row = pl.BlockSpec((1, sl), lambda n: (0, n))
    w_in_blk = pl.BlockSpec((1, 128, sl), lambda n: (n, 0, 0))
    w_out_blk = pl.BlockSpec((1, sl, 128), lambda n: (n, 0, 0))
    return pl.pallas_call(
        _s5_step_body,
        grid=(nt,),
        in_specs=[lane_blk, st_blk, st_blk, st_row, st_row, w_in_blk, w_in_blk, w_out_blk, w_out_blk,
                  pl.BlockSpec((1, 128), lambda n: (0, n))],
        out_specs=[lane_blk, st_blk, st_blk],
        out_shape=[jax.ShapeDtypeStruct((bd, dm), F32), jax.ShapeDtypeStruct((bd, g * p), F32),
                   jax.ShapeDtypeStruct((bd, g * p), F32)],
        compiler_params=_params(("parallel",)),
        name="s5_step",
    )(u, x0_re.reshape(bd, g * p), x0_im.reshape(bd, g * p), row(ab[0]), row(ab[1]),
      bdiag_in(bb[0]), bdiag_in(bb[1]), bdiag_out(c_re), bdiag_out(c_im), d.reshape(1, dm))


def _cumsum_body(nblk, tb, x_ref, o_ref):
    r = lax.broadcasted_iota(jnp.int32, (tb, tb), 0)
    c = lax.broadcasted_iota(jnp.int32, (tb, tb), 1)
    tri = jnp.where(c <= r, 1.0, 0.0).astype(F32)
    carry = jnp.zeros((1, x_ref.shape[-1]), F32)
    for i in range(nblk):
        blk = jnp.dot(tri, x_ref[0, i * tb:(i + 1) * tb, :], precision=HIGHEST, preferred_element_type=F32) + carry
        o_ref[0, i * tb:(i + 1) * tb, :] = blk
        carry = blk[tb - 1:tb, :]


def _cumsum_seq(x, tb=256):
    b, l, h = x.shape
    tb = min(tb, l)
    blk = pl.BlockSpec((1, l, h), lambda i: (i, 0, 0))
    return pl.pallas_call(
        functools.partial(_cumsum_body, l // tb, tb),
        grid=(b,),
        in_specs=[blk],
        out_specs=blk,
        out_shape=jax.ShapeDtypeStruct((b, l, h), F32),
        compiler_params=_params(("parallel",)),
        name="logf_cumsum",
    )(x)


def _fox_prompt_body(bq, bk, dh, q_ref, k_ref, v_ref, c_ref, ct_ref, o_ref, m_ref, l_ref, acc_ref):
    h = pl.program_id(1)
    i = pl.program_id(2)
    scale = dh ** -0.5
    q = q_ref[0]
    c_blk = c_ref[0]
    head = lax.broadcasted_iota(jnp.int32, c_blk.shape, 1)
    c_q = jnp.sum(jnp.where(head == h, c_blk, 0.0), axis=-1, keepdims=True)
    q_pos = i * bq + lax.broadcasted_iota(jnp.int32, (bq, bk), 0)
    k_off = lax.broadcasted_iota(jnp.int32, (bq, bk), 1)
    m_ref[...] = jnp.full(m_ref.shape, -jnp.inf, F32)
    l_ref[...] = jnp.zeros(l_ref.shape, F32)
    acc_ref[...] = jnp.zeros(acc_ref.shape, F32)

    def step(j, carry):
        start = pl.multiple_of(j * bk, bk)
        kb = k_ref[0, pl.ds(start, bk), :]
        vb = v_ref[0, pl.ds(start, bk), :]
        c_k = ct_ref[0, 0, pl.ds(j, 1), :]
        s = lax.dot_general(q, kb, (((1,), (1,)), ((), ())), preferred_element_type=F32) * scale + c_q - c_k
        s = jnp.where(j * bk + k_off <= q_pos, s, -jnp.inf)
        m_old = m_ref[...]
        m_new = jnp.maximum(m_old, jnp.max(s, axis=-1, keepdims=True))
        alpha = jnp.exp(m_old - m_new)
        pr = jnp.exp(s - m_new)
        l_ref[...] = alpha * l_ref[...] + jnp.sum(pr, axis=-1, keepdims=True)
        acc_ref[...] = alpha * acc_ref[...] + jnp.dot(pr.astype(BF16), vb, preferred_element_type=F32)
        m_ref[...] = m_new
        return carry

    n_kv = (i * bq + bq + bk - 1) // bk
    lax.fori_loop(0, n_kv, step, 0)
    o_ref[0] = (acc_ref[...] / l_ref[...]).astype(o_ref.dtype)


def _fox_prompt(q, k, v, c, n_heads, bq=512, bk=512):
    b, l, d = q.shape
    dh = d // n_heads
    bq = min(bq, l)
    bk = min(bk, l)
    ct = jnp.swapaxes(c, 1, 2).reshape(b, n_heads, l // bk, bk)
    body = functools.partial(_fox_prompt_body, bq, bk, dh)
    return pl.pallas_call(
        body,
        grid=(b, n_heads, l // bq),
        in_specs=[
            pl.BlockSpec((1, bq, dh), lambda bi, h, i: (bi, i, h)),
            pl.BlockSpec((1, l, dh), lambda bi, h, i: (bi, 0, h)),
            pl.BlockSpec((1, l, dh), lambda bi, h, i: (bi, 0, h)),
            pl.BlockSpec((1, bq, n_heads), lambda bi, h, i: (bi, i, 0)),
            pl.BlockSpec((1, 1, l // bk, bk), lambda bi, h, i: (bi, h, 0, 0)),
        ],
        out_specs=pl.BlockSpec((1, bq, dh), lambda bi, h, i: (bi, i, h)),
        out_shape=jax.ShapeDtypeStruct((b, l, d), BF16),
        scratch_shapes=[pltpu.VMEM((bq, 1), F32), pltpu.VMEM((bq, 1), F32), pltpu.VMEM((bq, dh), F32)],
        compiler_params=_params(("parallel", "parallel", "arbitrary")),
        name="fox_prompt",
    )(q, k, v, c, ct)


def _suffix_body(ps, pt_ref, lf_ref, o_ref, carry_ref):
    @pl.when(pl.program_id(1) == 0)
    def _():
        carry_ref[...] = jnp.zeros(carry_ref.shape, F32)

    lp = lf_ref[0]
    r = lax.broadcasted_iota(jnp.int32, (ps, ps), 0)
    c = lax.broadcasted_iota(jnp.int32, (ps, ps), 1)
    upper = jnp.where(c > r, 1.0, 0.0).astype(F32)
    after = jnp.dot(upper, lp, precision=HIGHEST, preferred_element_type=F32)
    o_ref[0, 0] = after + carry_ref[...]
    carry_ref[...] += jnp.sum(lp, axis=0, keepdims=True)


def _suffix_logf(cache_logf, page_table):
    bd, n_pages = page_table.shape
    _, ps, h = cache_logf.shape
    grid_spec = pltpu.PrefetchScalarGridSpec(
        num_scalar_prefetch=1,
        grid=(bd, n_pages),
        in_specs=[pl.BlockSpec((1, ps, h), lambda b, p, pt: (pt[b, n_pages - 1 - p], 0, 0))],
        out_specs=pl.BlockSpec((1, 1, ps, h), lambda b, p, pt: (b, n_pages - 1 - p, 0, 0)),
        scratch_shapes=[pltpu.VMEM((1, h), F32)],
    )
    return pl.pallas_call(
        functools.partial(_suffix_body, ps),
        grid_spec=grid_spec,
        out_shape=jax.ShapeDtypeStruct((bd, n_pages, ps, h), F32),
        compiler_params=_params(("parallel", "arbitrary")),
        name="logf_suffix",
    )(page_table, cache_logf)


def _fox_decode_body(n_heads, dh, pt_ref, q_ref, kn_ref, vn_ref, cn_ref, k_ref, v_ref, sfx_ref, o_ref,
                     m_ref, l_ref, acc_ref):
    p = pl.program_id(1)
    scale = dh ** -0.5
    q = q_ref[0].astype(BF16)

    @pl.when(p == 0)
    def _():
        kn = kn_ref[0].astype(BF16).astype(F32)
        m_ref[...] = jnp.sum(q.astype(F32) * kn, axis=-1, keepdims=True) * scale
        l_ref[...] = jnp.ones(l_ref.shape, F32)
        acc_ref[...] = vn_ref[0]

    kp = k_ref[0].astype(BF16)
    vp = v_ref[0].astype(BF16)
    n_rows = kp.shape[0]
    s = lax.dot_general(q, kp, (((1,), (1,)), ((), ())), preferred_element_type=F32)
    s = s * scale + cn_ref[0] + sfx_ref[0, 0]
    row_head = lax.broadcasted_iota(jnp.int32, (n_heads, n_rows), 0)
    col_head = lax.broadcasted_iota(jnp.int32, (n_heads, n_rows), 1) % n_heads
    s = jnp.where(row_head == col_head, s, -jnp.inf)
    m_old = m_ref[...]
    m_new = jnp.maximum(m_old, jnp.max(s, axis=-1, keepdims=True))
    alpha = jnp.exp(m_old - m_new)
    pr = jnp.exp(s - m_new)
    l_ref[...] = alpha * l_ref[...] + jnp.sum(pr, axis=-1, keepdims=True)
    acc_ref[...] = alpha * acc_ref[...] + jnp.dot(pr.astype(BF16), vp, preferred_element_type=F32)
    m_ref[...] = m_new

    @pl.when(p == pl.num_programs(1) - 1)
    def _():
        o_ref[0] = acc_ref[...] / l_ref[...]


def _fox_decode(q, k_new, v_new, logf_new, cache_k, cache_v, suffix, page_table):
    bd, n_heads, dh = q.shape
    n_pool, ps = cache_k.shape[:2]
    n_pages = page_table.shape[1]
    rows = ps * n_heads
    tok = pl.BlockSpec((1, n_heads, dh), lambda b, p, pt: (b, 0, 0))
    page = pl.BlockSpec((1, rows, dh), lambda b, p, pt: (pt[b, p], 0, 0))
    grid_spec = pltpu.PrefetchScalarGridSpec(
        num_scalar_prefetch=1,
        grid=(bd, n_pages),
        in_specs=[tok, tok, tok, pl.BlockSpec((1, n_heads, 1), lambda b, p, pt: (b, 0, 0)), page, page,
                  pl.BlockSpec((1, 1, 1, rows), lambda b, p, pt: (b, p, 0, 0))],
        out_specs=tok,
        scratch_shapes=[pltpu.VMEM((n_heads, 1), F32), pltpu.VMEM((n_heads, 1), F32),
                        pltpu.VMEM((n_heads, dh), F32)],
    )
    return pl.pallas_call(
        functools.partial(_fox_decode_body, n_heads, dh),
        grid_spec=grid_spec,
        out_shape=jax.ShapeDtypeStruct((bd, n_heads, dh), F32),
        compiler_params=_params(("parallel", "arbitrary")),
        name="fox_decode",
    )(page_table, q, k_new, v_new, logf_new.reshape(bd, n_heads, 1),
      cache_k.reshape(n_pool, rows, dh), cache_v.reshape(n_pool, rows, dh),
      suffix.reshape(bd, n_pages, 1, rows))


def _ffn(h, g_norm, w_gate_up, w_down, d_ff, t):
    xn = _rmsnorm(h, g_norm, t["act"], t["bm_norm"])
    (a,) = _matmul(xn, w_gate_up, d_ff, (0, d_ff), _epi_swiglu, (t["act"],), bm=t["bm"], bn=t["bn_ff"])
    (h,) = _matmul(a, w_down, h.shape[1], (0,), _epi_residual, (F32,), bm=t["bm"], bn=t["bn_down"],
                   bk=t["bk_down"], tiles=(h,))
    return h


def _trunk_front(x, w, s5, t, prompt_batch):
    m, dm = x.shape
    xn = _rmsnorm(x, w["norm_mix"][0], t["act"], t["bm_norm"])
    (u,) = _matmul(xn, w["ssm_w_in"], dm, (0,), _epi_plain, (F32,), bm=t["bm"], bn=t["bn"])
    g_act, fin = s5(u)
    (h,) = _matmul(g_act, w["ssm_w_glu"], dm, (0, dm), _epi_glu_residual, (F32,), bm=t["bm"], bn=t["bn_glu"],
                   tiles=(x,))
    h = _ffn(h, w["norm_ffn"][0], w["ffn_w_gate_up"][0], w["ffn_w_down"][0], w["d_ff"], t)
    z = _rmsnorm(h, w["norm_kv"], t["act"], t["bm_norm"])
    k, kb = _matmul(z, w["kv_w_k"], dm, (0,), _epi_dup, (F32, t["act"]), bm=t["bm"], bn=t["bn"])
    v, vb = _matmul(z, w["kv_w_v"], dm, (0,), _epi_dup, (F32, t["act"]), bm=t["bm"], bn=t["bn"])
    n_heads = w["kv_w_f"].shape[1]
    (logf,) = _matmul(z, w["kv_w_f"], n_heads, (0,), _epi_log_sigmoid, (F32,), bm=t["bm"], bn=n_heads,
                      rows=(w["kv_b_f"].reshape(1, n_heads),))
    qn = _rmsnorm(h, w["norm_mix"][1], t["act"], t["bm_norm"])
    (q,) = _matmul(qn, w["attn_w_q"], dm, (0,), _epi_plain, (t["act"],), bm=t["bm"], bn=t["bn"])
    return h, q, k, v, kb, vb, logf, fin


def _trunk_back(h, att, w, t):
    (h,) = _matmul(att, w["attn_w_o"], h.shape[1], (0,), _epi_residual, (F32,), bm=t["bm"], bn=t["bn"], tiles=(h,))
    h = _ffn(h, w["norm_ffn"][1], w["ffn_w_gate_up"][1], w["ffn_w_down"][1], w["d_ff"], t)
    return _rmsnorm(h, w["norm_final"], F32, t["bm_norm"])


def _down_k_block(d_ff):
    units = d_ff // 128
    for parts in range(2, units + 1):
        if units % parts == 0:
            return d_ff // parts
    return d_ff


def kernel(x_prompt, x_sample, state_ssm_re, state_ssm_im, cache_k, cache_v, cache_logf, page_table, norm_mix, norm_ffn, ssm_w_in, ssm_lambda_re, ssm_lambda_im, ssm_log_dt, ssm_b_re, ssm_b_im, ssm_c_re, ssm_c_im, ssm_d, ssm_w_glu, attn_w_q, attn_w_o, norm_kv, kv_w_k, kv_w_v, kv_w_f, kv_b_f, ffn_w_gate_up, ffn_w_down, norm_final):
    b, l, dm = x_prompt.shape
    bd, t_dec, _ = x_sample.shape
    assert t_dec == 1, "the sample group decodes one token per sequence"
    assert ssm_w_in.shape[0] == 1 and attn_w_q.shape[0] == 1, "one S5 layer followed by one FoX layer"
    g, p = ssm_lambda_re.shape[1:]
    c = dm // g
    n_heads = kv_w_f.shape[1]
    dh = dm // n_heads
    d_ff = ffn_w_down.shape[1]
    tchunk = SSM_CHUNK
    nc = l // tchunk
    n_sq = max(1, (nc - 1).bit_length())

    w = dict(
        norm_mix=norm_mix, norm_ffn=norm_ffn, norm_kv=norm_kv, norm_final=norm_final, kv_b_f=kv_b_f, d_ff=d_ff,
        ssm_w_in=ssm_w_in[0].astype(BF16), ssm_w_glu=ssm_w_glu[0].astype(BF16),
        attn_w_q=attn_w_q[0].astype(BF16), attn_w_o=attn_w_o[0].astype(BF16),
        kv_w_k=kv_w_k.astype(BF16), kv_w_v=kv_w_v.astype(BF16), kv_w_f=kv_w_f.astype(BF16),
        ffn_w_gate_up=ffn_w_gate_up.astype(BF16), ffn_w_down=ffn_w_down.astype(BF16),
    )
    bk_down = _down_k_block(d_ff)
    bn_ff = math.gcd(d_ff, 1024)
    tile_p = dict(act=BF16, bm=1024, bn=512, bn_glu=512, bn_ff=bn_ff, bn_down=512, bk_down=bk_down, bm_norm=256)
    tile_s = dict(act=F32, bm=bd, bn=1024, bn_glu=1024, bn_ff=bn_ff, bn_down=1024, bk_down=bk_down, bm_norm=bd)

    ab, bb, kt, ws_re, ws_im, wy_re, wy_im, pw = _s5_prep(
        ssm_lambda_re[0], ssm_lambda_im[0], ssm_log_dt[0], ssm_b_re[0], ssm_b_im[0], ssm_c_re[0], ssm_c_im[0], n_sq)
    tz, ws, wy = _s5_chunk_weights(kt, ws_re, ws_im, wy_re, wy_im)
    pw2 = jnp.concatenate([pw, pw], axis=-1).transpose(2, 0, 1, 3).reshape(g, 2 * n_sq, 2 * p)

    def s5_prompt(u):
        xg = u.reshape(b, nc, tchunk, g, c).transpose(3, 0, 1, 2, 4).reshape(g, b * nc, tchunk * c).astype(BF16)
        x0 = jnp.zeros((g, b, 2 * p), F32)
        y, fin = _s5_scan(xg, tz, ws, wy, pw2, x0, b)
        y = y.reshape(g, b, nc, tchunk, c).transpose(1, 2, 3, 0, 4).reshape(b * l, dm)
        return _s5_post(y, u, ssm_d[0], BF16, 256), fin

    xp = x_prompt.reshape(b * l, dm)
    h, q, k, v, kb, vb, logf, fin = _trunk_front(xp, w, s5_prompt, tile_p, b)
    cum = _cumsum_seq(logf.reshape(b, l, n_heads))
    att = _fox_prompt(q.reshape(b, l, dm), kb.reshape(b, l, dm), vb.reshape(b, l, dm), cum, n_heads)
    y_prompt = _trunk_back(h, att.reshape(b * l, dm), w, tile_p).reshape(b, l, dm)
    fin = fin.transpose(1, 0, 2)
    ssm_re_prompt = fin[None, :, :, :p]
    ssm_im_prompt = fin[None, :, :, p:]
    k_prompt = k.reshape(b, l, n_heads, dh)
    v_prompt = v.reshape(b, l, n_heads, dh)
    logf_prompt = logf.reshape(b, l, n_heads)

    def s5_sample(u):
        g_act, s_re, s_im = _s5_step(u, state_ssm_re[0], state_ssm_im[0], ab, bb, ssm_c_re[0], ssm_c_im[0], ssm_d[0])
        return g_act, (s_re, s_im)

    xs = x_sample.reshape(bd, dm)
    hs, qs, ks, vs, _, _, logfs, (s_re, s_im) = _trunk_front(xs, w, s5_sample, tile_s, bd)
    suffix = _suffix_logf(cache_logf, page_table)
    att_s = _fox_decode(qs.reshape(bd, n_heads, dh), ks.reshape(bd, n_heads, dh), vs.reshape(bd, n_heads, dh),
                        logfs, cache_k, cache_v, suffix, page_table)
    y_sample = _trunk_back(hs, att_s.reshape(bd, dm), w, tile_s).reshape(bd, 1, dm)
    ssm_re_sample = s_re.reshape(1, bd, g, p)
    ssm_im_sample = s_im.reshape(1, bd, g, p)
    k_sample = ks.reshape(bd, 1, n_heads, dh)
    v_sample = vs.reshape(bd, 1, n_heads, dh)
    logf_sample = logfs.reshape(bd, 1, n_heads)

    return (y_prompt, y_sample, ssm_re_prompt, ssm_im_prompt, k_prompt, v_prompt, logf_prompt,
            ssm_re_sample, ssm_im_sample, k_sample, v_sample, logf_sample)
```

```python
import functools
import math

import jax
import jax.numpy as jnp
from jax import lax
from jax.experimental import pallas as pl
from jax.experimental.pallas import tpu as pltpu

F32 = jnp.float32
BF16 = jnp.bfloat16
RMS_EPS = 1e-6
LANES = 128
SSM_CHUNK = 8
FOX_ROW_SPLIT = 2
V7X_VMEM_LIMIT = 56 * 1024 * 1024
HIGHEST = lax.Precision.HIGHEST
LOG2E = math.log2(math.e)
NT_DIMS = (((1,), (1,)), ((), ()))


def _params(semantics, vmem_bytes=V7X_VMEM_LIMIT):
    return pltpu.CompilerParams(dimension_semantics=semantics, vmem_limit_bytes=vmem_bytes)


def _sigmoid(x):
    return 1.0 / (1.0 + jnp.exp(-x))


def _rmsnorm_body(x_ref, g_ref, o_ref):
    x = x_ref[...]
    y = x * lax.rsqrt(jnp.mean(x * x, axis=-1, keepdims=True) + RMS_EPS)
    o_ref[...] = (y * g_ref[...]).astype(o_ref.dtype)


def _rmsnorm(x, g, out_dtype, bm):
    m, d = x.shape
    bm = min(bm, m)
    return pl.pallas_call(
        _rmsnorm_body,
        grid=(m // bm,),
        in_specs=[pl.BlockSpec((bm, d), lambda i: (i, 0)), pl.BlockSpec((1, d), lambda i: (0, 0))],
        out_specs=pl.BlockSpec((bm, d), lambda i: (i, 0)),
        out_shape=jax.ShapeDtypeStruct((m, d), out_dtype),
        compiler_params=_params(("parallel",)),
        name="rmsnorm",
    )(x, g.reshape(1, d))


def _mm_body(nw, n_tile, n_row, n_out, nk, epi, *refs):
    x_ref = refs[0]
    w_refs = refs[1:1 + nw]
    t_refs = refs[1 + nw:1 + nw + n_tile]
    r_refs = refs[1 + nw + n_tile:1 + nw + n_tile + n_row]
    o_refs = refs[1 + nw + n_tile + n_row:1 + nw + n_tile + n_row + n_out]
    acc_refs = refs[1 + nw + n_tile + n_row + n_out:]
    x = x_ref[...].astype(BF16)
    parts = [jnp.dot(x, w[...], preferred_element_type=F32) for w in w_refs]

    def finish(accs):
        outs = epi(accs, [t[...] for t in t_refs], [r[...] for r in r_refs])
        for o, v in zip(o_refs, outs):
            o[...] = v.astype(o.dtype)

    if nk == 1:
        finish(parts)
    else:
        k = pl.program_id(2)

        @pl.when(k == 0)
        def _():
            for a, p in zip(acc_refs, parts):
                a[...] = p

        @pl.when(jnp.logical_and(k > 0, k < nk - 1))
        def _():
            for a, p in zip(acc_refs, parts):
                a[...] += p

        @pl.when(k == nk - 1)
        def _():
            finish([a[...] + p for a, p in zip(acc_refs, parts)])


def _matmul(x, w, n_cols, col_offsets, epi, out_dtypes, *, bm, bn, bk=None, tiles=(), rows=()):
    m, kdim = x.shape
    bm = min(bm, m)
    bn = min(bn, n_cols)
    bk = kdim if bk is None else bk
    assert m % bm == 0 and n_cols % bn == 0 and kdim % bk == 0, (x.shape, w.shape, bm, bn, bk)
    assert all(off % bn == 0 for off in col_offsets)
    nk = kdim // bk
    nw = len(col_offsets)
    in_specs = [pl.BlockSpec((bm, bk), lambda i, j, k: (i, k))]
    for off in col_offsets:
        in_specs.append(pl.BlockSpec((bk, bn), functools.partial(lambda i, j, k, o: (k, j + o), o=off // bn)))
    in_specs += [pl.BlockSpec((bm, bn), lambda i, j, k: (i, j)) for _ in tiles]
    in_specs += [pl.BlockSpec((1, bn), lambda i, j, k: (0, j)) for _ in rows]
    body = functools.partial(_mm_body, nw, len(tiles), len(rows), len(out_dtypes), nk, epi)
    outs = pl.pallas_call(
        body,
        grid=(m // bm, n_cols // bn, nk),
        in_specs=in_specs,
        out_specs=[pl.BlockSpec((bm, bn), lambda i, j, k: (i, j)) for _ in out_dtypes],
        out_shape=[jax.ShapeDtypeStruct((m, n_cols), dt) for dt in out_dtypes],
        scratch_shapes=[pltpu.VMEM((bm, bn), F32) for _ in range(nw if nk > 1 else 0)],
        compiler_params=_params(("parallel", "parallel", "arbitrary")),
        name="matmul_" + getattr(epi, "__name__", "epi").strip("_"),
    )(x, *([w] * nw), *tiles, *rows)
    return outs


def _epi_plain(accs, tiles, rows):
    return (accs[0],)


def _epi_dup(accs, tiles, rows):
    return (accs[0], accs[0])


def _epi_scaled(scale, accs, tiles, rows):
    return (accs[0] * scale,)


def _epi_residual(accs, tiles, rows):
    return (tiles[0] + accs[0],)


def _epi_swiglu(accs, tiles, rows):
    g, u = accs
    return (g * _sigmoid(g) * u,)


def _epi_glu_residual(accs, tiles, rows):
    za, zb = accs
    return (tiles[0] + za * _sigmoid(zb),)


def _epi_log_sigmoid(accs, tiles, rows):
    z = accs[0] + rows[0]
    return (jnp.minimum(z, 0.0) - jnp.log1p(jnp.exp(-jnp.abs(z))),)


def _cmul(ar, ai, br, bi):
    return ar * br - ai * bi, ar * bi + ai * br


def _s5_prep_body(t, c, n_sq, lre_ref, lim_ref, ldt_ref, btr_ref, bti_ref, cre_ref, cim_ref,
                  tz_ref, ws_ref, wy_ref, pws_ref, ab_ref, bb_ref):
    rows, p = lre_ref.shape
    sw = ws_ref.shape[-1]
    lam_re = lre_ref[...]
    lam_im = lim_ref[...]
    dt = jnp.exp(ldt_ref[...])
    mag = jnp.exp(lam_re * dt)
    ab_re = mag * jnp.cos(lam_im * dt)
    ab_im = mag * jnp.sin(lam_im * dt)
    den = lam_re * lam_re + lam_im * lam_im
    co_re = ((ab_re - 1.0) * lam_re + ab_im * lam_im) / den
    co_im = (ab_im * lam_re - (ab_re - 1.0) * lam_im) / den
    bb_re, bb_im = _cmul(co_re, co_im, btr_ref[...], bti_ref[...])
    c_re = cre_ref[...]
    c_im = cim_ref[...]
    ab_ref[0] = ab_re
    ab_ref[1] = ab_im
    bb_ref[0] = bb_re
    bb_ref[1] = bb_im

    pw = [(jnp.ones_like(ab_re), jnp.zeros_like(ab_im))]
    for _ in range(t):
        pw.append(_cmul(pw[-1][0], pw[-1][1], ab_re, ab_im))

    def iota(shape, dim):
        return lax.broadcasted_iota(jnp.int32, shape, dim)

    same_group = iota((rows, rows), 0) // c == iota((rows, rows), 1) // c
    row_in_col_group = iota((rows, sw), 0) // c == iota((rows, sw), 1) // (2 * p)
    first_row_of_group = jnp.logical_and(row_in_col_group, iota((rows, sw), 0) % c == 0)

    def dot_nt_hi(a, b):
        return lax.dot_general(a, b, NT_DIMS, precision=HIGHEST, preferred_element_type=F32)

    def spread(w_re, w_im):
        packed = jnp.concatenate([w_re, w_im], axis=1)
        return jnp.where(row_in_col_group, jnp.concatenate([packed] * (sw // (2 * p)), axis=1), 0.0)

    tz_ref[0] = jnp.zeros(tz_ref.shape[1:], tz_ref.dtype)
    for d in range(t):
        pr, pi = pw[d]
        m_re, m_im = _cmul(c_re, c_im, pr, pi)
        kd_t = jnp.where(same_group, dot_nt_hi(bb_re, m_re) - dot_nt_hi(bb_im, m_im), 0.0).astype(tz_ref.dtype)
        for tau in range(t - d):
            tz_ref[0, tau * rows:(tau + 1) * rows, (tau + d) * rows:(tau + d + 1) * rows] = kd_t
        w_re, w_im = _cmul(bb_re, bb_im, pr, pi)
        tau = t - 1 - d
        ws_ref[0, tau * rows:(tau + 1) * rows, :] = spread(w_re, w_im).astype(ws_ref.dtype)
        qr, qi = pw[d + 1]
        y_re, y_im = _cmul(c_re, c_im, qr, qi)
        wy_ref[0, :, d * rows:(d + 1) * rows] = spread(y_re, -y_im).T.astype(wy_ref.dtype)

    def group_row(w):
        return jnp.sum(jnp.where(first_row_of_group, spread(w, w), 0.0), axis=0, keepdims=True)

    qr, qi = pw[t]
    for k in range(n_sq):
        pws_ref[0, k:k + 1, :] = group_row(qr)
        pws_ref[0, n_sq + k:n_sq + k + 1, :] = group_row(qi)
        qr, qi = _cmul(qr, qi, qr, qi)


def _s5_prep(lam_re, lam_im, log_dt, b_re, b_im, c_re, c_im, n_sq):
    g, p = lam_re.shape
    c = c_re.shape[1]
    t = SSM_CHUNK
    rows = LANES
    gl = rows // c
    nt = g // gl
    sw = gl * 2 * p
    per_row = lambda a: jnp.repeat(a, c, axis=0)
    bt_re = jnp.swapaxes(b_re, 1, 2).reshape(g * c, p)
    bt_im = jnp.swapaxes(b_im, 1, 2).reshape(g * c, p)
    rp = pl.BlockSpec((rows, p), lambda n: (n, 0))
    out_shapes = [
        jax.ShapeDtypeStruct((nt, t * rows, t * rows), BF16),
        jax.ShapeDtypeStruct((nt, t * rows, sw), BF16),
        jax.ShapeDtypeStruct((nt, sw, t * rows), BF16),
        jax.ShapeDtypeStruct((nt, 2 * n_sq, sw), F32),
        jax.ShapeDtypeStruct((2, g * c, p), F32),
        jax.ShapeDtypeStruct((2, g * c, p), F32),
    ]
    out_specs = [
        pl.BlockSpec((1, t * rows, t * rows), lambda n: (n, 0, 0)),
        pl.BlockSpec((1, t * rows, sw), lambda n: (n, 0, 0)),
        pl.BlockSpec((1, sw, t * rows), lambda n: (n, 0, 0)),
        pl.BlockSpec((1, 2 * n_sq, sw), lambda n: (n, 0, 0)),
        pl.BlockSpec((2, rows, p), lambda n: (0, n, 0)),
        pl.BlockSpec((2, rows, p), lambda n: (0, n, 0)),
    ]
    tz, ws, wy, pws, ab_rows, bb_rows = pl.pallas_call(
        functools.partial(_s5_prep_body, t, c, n_sq),
        grid=(nt,),
        in_specs=[rp, rp, pl.BlockSpec((rows, 1), lambda n: (n, 0)), rp, rp, rp, rp],
        out_specs=out_specs,
        out_shape=out_shapes,
        compiler_params=_params(("parallel",)),
        name="s5_prep",
    )(per_row(lam_re), per_row(lam_im), per_row(log_dt.reshape(g, 1)), bt_re, bt_im,
      c_re.reshape(g * c, p), c_im.reshape(g * c, p))
    ab = ab_rows.reshape(2, g, c, p)[:, :, 0, :]
    bb = bb_rows.reshape(2, g, c, p)
    return tz, ws, wy, pws, ab, bb


def _gelu_tanh(y):
    return 0.5 * y * (1.0 + jnp.tanh(math.sqrt(2.0 / math.pi) * (y + 0.044715 * (y * y * y))))


def _s5_tile_body(t, p, n_sq, u_ref, tz_ref, ws_ref, wy_ref, pws_ref, x0_ref, d_ref, o_ref, fin_ref, yt_ref):
    l, lanes = u_ref.shape
    nc = l // t
    sw = ws_ref.shape[-1]
    ustk = jnp.concatenate(
        [u_ref[pl.ds(tau, nc, stride=t), :].astype(BF16) for tau in range(t)], axis=1)
    e = jnp.dot(ustk, ws_ref[0], preferred_element_type=F32)
    row = lax.broadcasted_iota(jnp.int32, (nc, 2 * p), 0)
    lane = lax.broadcasted_iota(jnp.int32, (nc, 2 * p), 1)
    sign = jnp.where(lane < p, -1.0, 1.0)

    def cmul_packed(ar, ai, z):
        return ar * z + (ai * sign) * pltpu.roll(z, p, 1)

    s_in = []
    for gi in range(sw // (2 * p)):
        cols = slice(gi * 2 * p, (gi + 1) * 2 * p)
        ar = [pws_ref[0, k:k + 1, cols] for k in range(n_sq)]
        ai = [pws_ref[0, n_sq + k:n_sq + k + 1, cols] for k in range(n_sq)]
        x0 = x0_ref[0, 0, :, cols]
        z = e[:, cols]
        z = z + jnp.where(row == 0, cmul_packed(ar[0], ai[0], jnp.broadcast_to(x0, (nc, 2 * p))), 0.0)
        for k in range(n_sq):
            sh = 1 << k
            zs = jnp.where(row >= sh, pltpu.roll(z, sh, 0), 0.0)
            z = z + cmul_packed(ar[k], ai[k], zs)
        fin_ref[0, 0, :, cols] = z[nc - 1:nc, :]
        s_in.append(jnp.where(row == 0, x0, pltpu.roll(z, 1, 0)).astype(BF16))
    s_in = jnp.concatenate(s_in, axis=1)
    ystk = (jnp.dot(ustk, tz_ref[0], preferred_element_type=F32)
            + jnp.dot(s_in, wy_ref[0], preferred_element_type=F32))
    for tau in range(t):
        yt_ref[pl.ds(tau, nc, stride=t), :] = ystk[:, tau * lanes:(tau + 1) * lanes]
    o_ref[...] = _gelu_tanh(yt_ref[...] + d_ref[...] * u_ref[...]).astype(o_ref.dtype)


def _s5_prompt(u, tz, ws, wy, pws, x0, d, nb, p):
    m, dm = u.shape
    l = m // nb
    nt, _, sw = ws.shape
    t = SSM_CHUNK
    n_sq = pws.shape[1] // 2
    assert dm == nt * LANES and l % (8 * t) == 0 and (1 << n_sq) >= l // t
    body = functools.partial(_s5_tile_body, t, p, n_sq)
    return pl.pallas_call(
        body,
        grid=(nt, nb),
        in_specs=[
            pl.BlockSpec((l, LANES), lambda n, b: (b, n)),
            pl.BlockSpec((1, t * LANES, t * LANES), lambda n, b: (n, 0, 0)),
            pl.BlockSpec((1, t * LANES, sw), lambda n, b: (n, 0, 0)),
            pl.BlockSpec((1, sw, t * LANES), lambda n, b: (n, 0, 0)),
            pl.BlockSpec((1, 2 * n_sq, sw), lambda n, b: (n, 0, 0)),
            pl.BlockSpec((1, 1, 1, sw), lambda n, b: (b, n, 0, 0)),
            pl.BlockSpec((1, LANES), lambda n, b: (0, n)),
        ],
        out_specs=[
            pl.BlockSpec((l, LANES), lambda n, b: (b, n)),
            pl.BlockSpec((1, 1, 1, sw), lambda n, b: (b, n, 0, 0)),
        ],
        out_shape=[jax.ShapeDtypeStruct((m, dm), BF16), jax.ShapeDtypeStruct((nb, nt, 1, sw), F32)],
        scratch_shapes=[pltpu.VMEM((l, LANES), F32)],
        compiler_params=_params(("parallel", "parallel")),
        name="s5_prompt",
    )(u, tz, ws, wy, pws, x0, d.reshape(1, dm))


def _s5_step_body(u_ref, x0r_ref, x0i_ref, abr_ref, abi_ref, wbr_ref, wbi_ref, wcr_ref, wci_ref, d_ref,
                  g_ref, sr_ref, si_ref):
    u = u_ref[...]
    ub = u.astype(BF16)
    bu_re = jnp.dot(ub, wbr_ref[0], preferred_element_type=F32)
    bu_im = jnp.dot(ub, wbi_ref[0], preferred_element_type=F32)
    ar, ai = abr_ref[...], abi_ref[...]
    s0r, s0i = _cmul(ar, ai, x0r_ref[...], x0i_ref[...])
    s_re = bu_re + s0r
    s_im = bu_im + s0i
    sr_ref[...] = s_re
    si_ref[...] = s_im
    y = (jnp.dot(s_re.astype(BF16), wcr_ref[0], preferred_element_type=F32)
         - jnp.dot(s_im.astype(BF16), wci_ref[0], preferred_element_type=F32)
         + d_ref[...] * u)
    g_ref[...] = _gelu_tanh(y)


def _s5_step(u, x0_re, x0_im, ab, bb, c_re, c_im, d):
    bd, dm = u.shape
    g, p = ab.shape[1:]
    c = dm // g
    gl = LANES // c
    nt = g // gl
    sl = gl * p
    eye = jnp.eye(gl, dtype=F32)

    def bdiag_in(w):
        w = w.reshape(nt, gl, c, p)
        return jnp.einsum("nacp,ab->nacbp", w, eye).reshape(nt, gl * c, gl * p).astype(BF16)

    def bdiag_out(w):
        w = w.reshape(nt, gl, c, p)
        return jnp.einsum("nacp,ab->napbc", w, eye).reshape(nt, gl * p, gl * c).astype(BF16)

    row = lambda a: a.reshape(1, g * p)
    lane_blk = pl.BlockSpec((bd, LANES), lambda n: (0, n))
    st_blk = pl.BlockSpec((bd, sl), lambda n: (0, n))
    st_row = pl.BlockSpec((1, sl), lambda n: (0, n))
    w_in_blk = pl.BlockSpec((1, LANES, sl), lambda n: (n, 0, 0))
    w_out_blk = pl.BlockSpec((1, sl, LANES), lambda n: (n, 0, 0))
    return pl.pallas_call(
        _s5_step_body,
        grid=(nt,),
        in_specs=[lane_blk, st_blk, st_blk, st_row, st_row, w_in_blk, w_in_blk, w_out_blk, w_out_blk,
                  pl.BlockSpec((1, LANES), lambda n: (0, n))],
        out_specs=[lane_blk, st_blk, st_blk],
        out_shape=[jax.ShapeDtypeStruct((bd, dm), F32), jax.ShapeDtypeStruct((bd, g * p), F32),
                   jax.ShapeDtypeStruct((bd, g * p), F32)],
        compiler_params=_params(("parallel",)),
        name="s5_step",
    )(u, x0_re.reshape(bd, g * p), x0_im.reshape(bd, g * p), row(ab[0]), row(ab[1]),
      bdiag_in(bb[0]), bdiag_in(bb[1]), bdiag_out(c_re), bdiag_out(c_im), d.reshape(1, dm))


def _cumsum_body(nblk, tb, scale, x_ref, o_ref):
    r = lax.broadcasted_iota(jnp.int32, (tb, tb), 0)
    c = lax.broadcasted_iota(jnp.int32, (tb, tb), 1)
    tri = jnp.where(c <= r, 1.0, 0.0).astype(F32)
    carry = jnp.zeros((1, x_ref.shape[-1]), F32)
    for i in range(nblk):
        blk = jnp.dot(tri, x_ref[0, i * tb:(i + 1) * tb, :], precision=HIGHEST, preferred_element_type=F32) + carry
        o_ref[0, i * tb:(i + 1) * tb, :] = blk * scale
        carry = blk[tb - 1:tb, :]


def _cumsum_seq(x, scale, tb=256):
    b, l, h = x.shape
    tb = min(tb, l)
    blk = pl.BlockSpec((1, l, h), lambda i: (i, 0, 0))
    return pl.pallas_call(
        functools.partial(_cumsum_body, l // tb, tb, scale),
        grid=(b,),
        in_specs=[blk],
        out_specs=blk,
        out_shape=jax.ShapeDtypeStruct((b, l, h), F32),
        compiler_params=_params(("parallel",)),
        name="logf_cumsum",
    )(x)


def _fox_prompt_body(blk, q_ref, k_ref, v_ref, c_ref, ct_ref, o_ref, m_ref, l_ref, acc_ref):
    h = pl.program_id(1)
    i = pl.program_id(2)
    sub = blk // FOX_ROW_SPLIT
    c_blk = c_ref[0]
    head = lax.broadcasted_iota(jnp.int32, c_blk.shape, 1)
    c_q_all = jnp.sum(jnp.where(head == h, c_blk, 0.0), axis=-1, keepdims=True)
    m_ref[...] = jnp.full(m_ref.shape, -jnp.inf, F32)
    l_ref[...] = jnp.zeros(l_ref.shape, F32)
    acc_ref[...] = jnp.zeros(acc_ref.shape, F32)

    def lane_tiles(x):
        return [x[:, t * LANES:(t + 1) * LANES] for t in range(x.shape[1] // LANES)]

    def kv_block(j, causal):
        start = pl.multiple_of(j * blk, blk)
        for r in range(FOX_ROW_SPLIT):
            rows = slice(r * sub, (r + 1) * sub)
            n_keys = (r + 1) * sub if causal else blk
            q = q_ref[0, rows, :]
            c_q = c_q_all[rows]
            kb = k_ref[0, pl.ds(start, n_keys), :]
            vb = v_ref[0, pl.ds(start, n_keys), :]
            c_k = ct_ref[0, 0, pl.ds(j, 1), :][:, :n_keys]
            s = lax.dot_general(q, kb, NT_DIMS, preferred_element_type=F32) - c_k
            if causal:
                q_pos = r * sub + lax.broadcasted_iota(jnp.int32, (sub, n_keys), 0)
                k_pos = lax.broadcasted_iota(jnp.int32, (sub, n_keys), 1)
                s = jnp.where(k_pos <= q_pos, s, -jnp.inf)
            s_max = jnp.max(functools.reduce(jnp.maximum, lane_tiles(s)), axis=-1, keepdims=True)
            m_old = m_ref[rows, :]
            m_new = jnp.maximum(m_old, s_max + c_q)
            alpha = jnp.exp2(m_old - m_new)
            pr = jnp.exp2(s - (m_new - c_q))
            p_sum = jnp.sum(functools.reduce(jnp.add, lane_tiles(pr)), axis=-1, keepdims=True)
            l_ref[rows, :] = alpha * l_ref[rows, :] + p_sum
            acc_ref[rows, :] = alpha * acc_ref[rows, :] + jnp.dot(pr.astype(BF16), vb, preferred_element_type=F32)
            m_ref[rows, :] = m_new

    def past_block(j, carry):
        kv_block(j, False)
        return carry

    lax.fori_loop(0, i, past_block, 0)
    kv_block(i, True)
    o_ref[0] = (acc_ref[...] / l_ref[...]).astype(o_ref.dtype)


def _fox_prompt(q, k, v, c, n_heads, blk=512):
    b, l, d = q.shape
    dh = d // n_heads
    blk = min(blk, l)
    ct = jnp.swapaxes(c, 1, 2).reshape(b, n_heads, l // blk, blk)
    return pl.pallas_call(
        functools.partial(_fox_prompt_body, blk),
        grid=(b, n_heads, l // blk),
        in_specs=[
            pl.BlockSpec((1, blk, dh), lambda bi, h, i: (bi, i, h)),
            pl.BlockSpec((1, l, dh), lambda bi, h, i: (bi, 0, h)),
            pl.BlockSpec((1, l, dh), lambda bi, h, i: (bi, 0, h)),
            pl.BlockSpec((1, blk, n_heads), lambda bi, h, i: (bi, i, 0)),
            pl.BlockSpec((1, 1, l // blk, blk), lambda bi, h, i: (bi, h, 0, 0)),
        ],
        out_specs=pl.BlockSpec((1, blk, dh), lambda bi, h, i: (bi, i, h)),
        out_shape=jax.ShapeDtypeStruct((b, l, d), BF16),
        scratch_shapes=[pltpu.VMEM((blk, 1), F32), pltpu.VMEM((blk, 1), F32), pltpu.VMEM((blk, dh), F32)],
        compiler_params=_params(("parallel", "parallel", "arbitrary")),
        name="fox_prompt",
    )(q, k, v, c, ct)


def _fox_decode_body(n_heads, dh, n_split, pt_ref, q_ref, kn_ref, vn_ref, cn_ref, *refs):
    k_refs = refs[:n_split]
    v_refs = refs[n_split:2 * n_split]
    lf_ref, o_ref, m_ref, l_ref, acc_ref, carry_ref = refs[2 * n_split:]
    step = pl.program_id(1)
    scale = dh ** -0.5
    q = q_ref[0].astype(BF16)

    @pl.when(step == 0)
    def _():
        kn = kn_ref[0].astype(BF16).astype(F32)
        m_ref[...] = jnp.sum(q.astype(F32) * kn, axis=-1, keepdims=True) * scale
        l_ref[...] = jnp.ones(l_ref.shape, F32)
        acc_ref[...] = vn_ref[0]
        carry_ref[...] = jnp.zeros(carry_ref.shape, F32)

    lf = lf_ref[0]
    n_r, lanes = lf.shape
    lane = lax.broadcasted_iota(jnp.int32, lf.shape, 1)
    same_head_total = lf
    later_in_row = jnp.zeros_like(lf)
    for k in range(1, lanes // n_heads):
        same_head_total = same_head_total + pltpu.roll(lf, k * n_heads, 1)
        later_in_row = later_in_row + jnp.where(lane + k * n_heads < lanes, pltpu.roll(lf, lanes - k * n_heads, 1), 0.0)
    r_i = lax.broadcasted_iota(jnp.int32, (n_r, n_r), 0)
    c_i = lax.broadcasted_iota(jnp.int32, (n_r, n_r), 1)
    later_rows = jnp.dot(jnp.where(c_i > r_i, 1.0, 0.0).astype(F32), same_head_total,
                         precision=HIGHEST, preferred_element_type=F32)
    decay = later_in_row + later_rows + carry_ref[...]
    carry_ref[...] += jnp.sum(same_head_total, axis=0, keepdims=True)

    s = jnp.concatenate([lax.dot_general(q, k_ref[0].astype(BF16), NT_DIMS, preferred_element_type=F32)
                         for k_ref in k_refs], axis=1)
    row_head = lax.broadcasted_iota(jnp.int32, (n_heads, lanes), 0)
    col_head = lax.broadcasted_iota(jnp.int32, (n_heads, lanes), 1) % n_heads
    own = row_head == col_head
    cn = cn_ref[0]
    s = jnp.concatenate(
        [jnp.where(own, s[:, r * lanes:(r + 1) * lanes] * scale + cn + decay[r:r + 1, :], -jnp.inf)
         for r in range(n_r)], axis=1)
    m_old = m_ref[...]
    m_new = jnp.maximum(m_old, jnp.max(s, axis=-1, keepdims=True))
    alpha = jnp.exp(m_old - m_new)
    pr = jnp.exp(s - m_new)
    l_ref[...] = alpha * l_ref[...] + jnp.sum(pr, axis=-1, keepdims=True)
    pb = pr.astype(BF16)
    slab = pb.shape[1] // n_split
    pv = sum(jnp.dot(pb[:, i * slab:(i + 1) * slab], v_ref[0].astype(BF16), preferred_element_type=F32)
             for i, v_ref in enumerate(v_refs))
    acc_ref[...] = alpha * acc_ref[...] + pv
    m_ref[...] = m_new

    @pl.when(step == pl.num_programs(1) - 1)
    def _():
        o_ref[0] = acc_ref[...] / l_ref[...]


def _fox_decode(q, k_new, v_new, logf_new, cache_k, cache_v, cache_logf, page_table, n_split=2):
    bd, n_heads, dh = q.shape
    n_pool, ps = cache_k.shape[:2]
    n_pages = page_table.shape[1]
    rows = ps * n_heads
    assert LANES % n_heads == 0 and rows % LANES == 0
    tok = pl.BlockSpec((1, n_heads, dh), lambda b, s, pt: (b, 0, 0))
    page_of = lambda b, s, pt: (pt[b, n_pages - 1 - s], 0, 0)
    slabs = [pl.BlockSpec((1, rows // n_split, dh),
                          functools.partial(lambda b, s, pt, i: (pt[b, n_pages - 1 - s], i, 0), i=i))
             for i in range(n_split)]
    grid_spec = pltpu.PrefetchScalarGridSpec(
        num_scalar_prefetch=1,
        grid=(bd, n_pages),
        in_specs=[tok, tok, tok, pl.BlockSpec((1, n_heads, 1), lambda b, s, pt: (b, 0, 0)), *slabs, *slabs,
                  pl.BlockSpec((1, rows // LANES, LANES), page_of)],
        out_specs=tok,
        scratch_shapes=[pltpu.VMEM((n_heads, 1), F32), pltpu.VMEM((n_heads, 1), F32),
                        pltpu.VMEM((n_heads, dh), F32), pltpu.VMEM((1, LANES), F32)],
    )
    return pl.pallas_call(
        functools.partial(_fox_decode_body, n_heads, dh, n_split),
        grid_spec=grid_spec,
        out_shape=jax.ShapeDtypeStruct((bd, n_heads, dh), F32),
        compiler_params=_params(("parallel", "arbitrary")),
        name="fox_decode",
    )(page_table, q, k_new, v_new, logf_new.reshape(bd, n_heads, 1),
      *([cache_k.reshape(n_pool, rows, dh)] * n_split), *([cache_v.reshape(n_pool, rows, dh)] * n_split),
      cache_logf.reshape(n_pool, rows // LANES, LANES))


def _ffn(h, g_norm, w_gate_up, w_down, d_ff, t):
    xn = _rmsnorm(h, g_norm, t["act"], t["bm_norm"])
    (a,) = _matmul(xn, w_gate_up, d_ff, (0, d_ff), _epi_swiglu, (t["act"],), bm=t["bm"], bn=t["bn_ff"])
    (h,) = _matmul(a, w_down, h.shape[1], (0,), _epi_residual, (F32,), bm=t["bm"], bn=t["bn_down"],
                   bk=t["bk_down"], tiles=(h,))
    return h


def _trunk_front(x, w, s5, t, q_scale):
    m, dm = x.shape
    xn = _rmsnorm(x, w["norm_mix"][0], t["act"], t["bm_norm"])
    (u,) = _matmul(xn, w["ssm_w_in"], dm, (0,), _epi_plain, (F32,), bm=t["bm"], bn=t["bn"])
    g_act, fin = s5(u)
    (h,) = _matmul(g_act, w["ssm_w_glu"], dm, (0, dm), _epi_glu_residual, (F32,), bm=t["bm"], bn=t["bn_glu"],
                   tiles=(x,))
    h = _ffn(h, w["norm_ffn"][0], w["ffn_w_gate_up"][0], w["ffn_w_down"][0], w["d_ff"], t)
    z = _rmsnorm(h, w["norm_kv"], t["act"], t["bm_norm"])
    k, kb = _matmul(z, w["kv_w_k"], dm, (0,), _epi_dup, (F32, t["act"]), bm=t["bm"], bn=t["bn"])
    v, vb = _matmul(z, w["kv_w_v"], dm, (0,), _epi_dup, (F32, t["act"]), bm=t["bm"], bn=t["bn"])
    n_heads = w["kv_w_f"].shape[1]
    (logf,) = _matmul(z, w["kv_w_f"], n_heads, (0,), _epi_log_sigmoid, (F32,), bm=t["bm"], bn=n_heads,
                      rows=(w["kv_b_f"].reshape(1, n_heads),))
    qn = _rmsnorm(h, w["norm_mix"][1], t["act"], t["bm_norm"])
    q_epi = _epi_plain if q_scale is None else functools.partial(_epi_scaled, q_scale)
    (q,) = _matmul(qn, w["attn_w_q"], dm, (0,), q_epi, (t["act"],), bm=t["bm"], bn=t["bn"])
    return h, q, k, v, kb, vb, logf, fin


def _trunk_back(h, att, w, t):
    (h,) = _matmul(att, w["attn_w_o"], h.shape[1], (0,), _epi_residual, (F32,), bm=t["bm"], bn=t["bn"], tiles=(h,))
    h = _ffn(h, w["norm_ffn"][1], w["ffn_w_gate_up"][1], w["ffn_w_down"][1], w["d_ff"], t)
    return _rmsnorm(h, w["norm_final"], F32, t["bm_norm"])


def _down_k_block(d_ff):
    units = d_ff // LANES
    for parts in range(2, units + 1):
        if units % parts == 0:
            return d_ff // parts
    return d_ff


def kernel(x_prompt, x_sample, state_ssm_re, state_ssm_im, cache_k, cache_v, cache_logf, page_table, norm_mix, norm_ffn, ssm_w_in, ssm_lambda_re, ssm_lambda_im, ssm_log_dt, ssm_b_re, ssm_b_im, ssm_c_re, ssm_c_im, ssm_d, ssm_w_glu, attn_w_q, attn_w_o, norm_kv, kv_w_k, kv_w_v, kv_w_f, kv_b_f, ffn_w_gate_up, ffn_w_down, norm_final):
    b, l, dm = x_prompt.shape
    bd, t_dec, _ = x_sample.shape
    assert t_dec == 1, "the sample group decodes one token per sequence"
    assert ssm_w_in.shape[0] == 1 and attn_w_q.shape[0] == 1, "one S5 layer followed by one FoX layer"
    g, p = ssm_lambda_re.shape[1:]
    n_heads = kv_w_f.shape[1]
    dh = dm // n_heads
    d_ff = ffn_w_down.shape[1]
    nc = l // SSM_CHUNK
    n_sq = max(1, (nc - 1).bit_length())

    w = dict(
        norm_mix=norm_mix, norm_ffn=norm_ffn, norm_kv=norm_kv, norm_final=norm_final, kv_b_f=kv_b_f, d_ff=d_ff,
        ssm_w_in=ssm_w_in[0].astype(BF16), ssm_w_glu=ssm_w_glu[0].astype(BF16),
        attn_w_q=attn_w_q[0].astype(BF16), attn_w_o=attn_w_o[0].astype(BF16),
        kv_w_k=kv_w_k.astype(BF16), kv_w_v=kv_w_v.astype(BF16), kv_w_f=kv_w_f.astype(BF16),
        ffn_w_gate_up=[ffn_w_gate_up[i].astype(BF16) for i in range(ffn_w_gate_up.shape[0])],
        ffn_w_down=[ffn_w_down[i].astype(BF16) for i in range(ffn_w_down.shape[0])],
    )
    bk_down = _down_k_block(d_ff)
    bn_ff = math.gcd(d_ff, 1024)
    tile_p = dict(act=BF16, bm=1024, bn=512, bn_glu=512, bn_ff=bn_ff, bn_down=512, bk_down=bk_down, bm_norm=256)
    tile_s = dict(act=F32, bm=bd, bn=1024, bn_glu=1024, bn_ff=bn_ff, bn_down=1024, bk_down=bk_down, bm_norm=bd)

    tz, ws, wy, pws, ab, bb = _s5_prep(
        ssm_lambda_re[0], ssm_lambda_im[0], ssm_log_dt[0], ssm_b_re[0], ssm_b_im[0], ssm_c_re[0], ssm_c_im[0], n_sq)
    nt, _, sw = ws.shape

    def s5_prompt(u):
        x0 = jnp.zeros((b, nt, 1, sw), F32)
        return _s5_prompt(u, tz, ws, wy, pws, x0, ssm_d[0], b, p)

    xp = x_prompt.reshape(b * l, dm)
    h, q, k, v, kb, vb, logf, fin = _trunk_front(xp, w, s5_prompt, tile_p, dh ** -0.5 * LOG2E)
    cum = _cumsum_seq(logf.reshape(b, l, n_heads), LOG2E)
    att = _fox_prompt(q.reshape(b, l, dm), kb.reshape(b, l, dm), vb.reshape(b, l, dm), cum, n_heads)
    y_prompt = _trunk_back(h, att.reshape(b * l, dm), w, tile_p).reshape(b, l, dm)
    fin = fin.reshape(b, g, 2 * p)
    ssm_re_prompt = fin[None, :, :, :p]
    ssm_im_prompt = fin[None, :, :, p:]
    k_prompt = k.reshape(b, l, n_heads, dh)
    v_prompt = v.reshape(b, l, n_heads, dh)
    logf_prompt = logf.reshape(b, l, n_heads)

    def s5_sample(u):
        g_act, s_re, s_im = _s5_step(u, state_ssm_re[0], state_ssm_im[0], ab, bb, ssm_c_re[0], ssm_c_im[0], ssm_d[0])
        return g_act, (s_re, s_im)

    xs = x_sample.reshape(bd, dm)
    hs, qs, ks, vs, _, _, logfs, (s_re, s_im) = _trunk_front(xs, w, s5_sample, tile_s, None)
    att_s = _fox_decode(qs.reshape(bd, n_heads, dh), ks.reshape(bd, n_heads, dh), vs.reshape(bd, n_heads, dh),
                        logfs, cache_k, cache_v, cache_logf, page_table)
    y_sample = _trunk_back(hs, att_s.reshape(bd, dm), w, tile_s).reshape(bd, 1, dm)
    ssm_re_sample = s_re.reshape(1, bd, g, p)
    ssm_im_sample = s_im.reshape(1, bd, g, p)
    k_sample = ks.reshape(bd, 1, n_heads, dh)
    v_sample = vs.reshape(bd, 1, n_heads, dh)
    logf_sample = logfs.reshape(bd, 1, n_heads)

    return (y_prompt, y_sample, ssm_re_prompt, ssm_im_prompt, k_prompt, v_prompt, logf_prompt,
            ssm_re_sample, ssm_im_sample, k_sample, v_sample, logf_sample)
```

```python
import functools
import math

import jax
import jax.numpy as jnp
from jax import lax
from jax.experimental import pallas as pl
from jax.experimental.pallas import tpu as pltpu

F32 = jnp.float32
BF16 = jnp.bfloat16
RMS_EPS = 1e-6
LANES = 128
SSM_CHUNK = 8
FOX_ROW_SPLIT = 4
V7X_VMEM_LIMIT = 56 * 1024 * 1024
HIGHEST = lax.Precision.HIGHEST
LOG2E = math.log2(math.e)
NT_DIMS = (((1,), (1,)), ((), ()))


def _params(semantics, vmem_bytes=V7X_VMEM_LIMIT):
    return pltpu.CompilerParams(dimension_semantics=semantics, vmem_limit_bytes=vmem_bytes)


def _sigmoid(x):
    return 1.0 / (1.0 + jnp.exp(-x))


def _rmsnorm_body(x_ref, g_ref, o_ref):
    x = x_ref[...]
    y = x * lax.rsqrt(jnp.mean(x * x, axis=-1, keepdims=True) + RMS_EPS)
    o_ref[...] = (y * g_ref[...]).astype(o_ref.dtype)


def _rmsnorm(x, g, out_dtype, bm):
    m, d = x.shape
    bm = min(bm, m)
    return pl.pallas_call(
        _rmsnorm_body,
        grid=(m // bm,),
        in_specs=[pl.BlockSpec((bm, d), lambda i: (i, 0)), pl.BlockSpec((1, d), lambda i: (0, 0))],
        out_specs=pl.BlockSpec((bm, d), lambda i: (i, 0)),
        out_shape=jax.ShapeDtypeStruct((m, d), out_dtype),
        compiler_params=_params(("parallel",)),
        name="rmsnorm",
    )(x, g.reshape(1, d))


def _mm_body(nw, n_tile, n_row, n_out, nk, epi, *refs):
    x_ref = refs[0]
    w_refs = refs[1:1 + nw]
    t_refs = refs[1 + nw:1 + nw + n_tile]
    r_refs = refs[1 + nw + n_tile:1 + nw + n_tile + n_row]
    o_refs = refs[1 + nw + n_tile + n_row:1 + nw + n_tile + n_row + n_out]
    acc_refs = refs[1 + nw + n_tile + n_row + n_out:]
    x = x_ref[...].astype(BF16)
    parts = [jnp.dot(x, w[...], preferred_element_type=F32) for w in w_refs]

    def finish(accs):
        outs = epi(accs, [t[...] for t in t_refs], [r[...] for r in r_refs])
        for o, v in zip(o_refs, outs):
            o[...] = v.astype(o.dtype)

    if nk == 1:
        finish(parts)
    else:
        k = pl.program_id(2)

        @pl.when(k == 0)
        def _():
            for a, p in zip(acc_refs, parts):
                a[...] = p

        @pl.when(jnp.logical_and(k > 0, k < nk - 1))
        def _():
            for a, p in zip(acc_refs, parts):
                a[...] += p

        @pl.when(k == nk - 1)
        def _():
            finish([a[...] + p for a, p in zip(acc_refs, parts)])


def _matmul(x, w, n_cols, col_offsets, epi, out_dtypes, *, bm, bn, bk=None, tiles=(), rows=()):
    m, kdim = x.shape
    bm = min(bm, m)
    bn = min(bn, n_cols)
    bk = kdim if bk is None else bk
    assert m % bm == 0 and n_cols % bn == 0 and kdim % bk == 0, (x.shape, w.shape, bm, bn, bk)
    assert all(off % bn == 0 for off in col_offsets)
    nk = kdim // bk
    nw = len(col_offsets)
    in_specs = [pl.BlockSpec((bm, bk), lambda i, j, k: (i, k))]
    for off in col_offsets:
        in_specs.append(pl.BlockSpec((bk, bn), functools.partial(lambda i, j, k, o: (k, j + o), o=off // bn)))
    in_specs += [pl.BlockSpec((bm, bn), lambda i, j, k: (i, j)) for _ in tiles]
    in_specs += [pl.BlockSpec((1, bn), lambda i, j, k: (0, j)) for _ in rows]
    body = functools.partial(_mm_body, nw, len(tiles), len(rows), len(out_dtypes), nk, epi)
    outs = pl.pallas_call(
        body,
        grid=(m // bm, n_cols // bn, nk),
        in_specs=in_specs,
        out_specs=[pl.BlockSpec((bm, bn), lambda i, j, k: (i, j)) for _ in out_dtypes],
        out_shape=[jax.ShapeDtypeStruct((m, n_cols), dt) for dt in out_dtypes],
        scratch_shapes=[pltpu.VMEM((bm, bn), F32) for _ in range(nw if nk > 1 else 0)],
        compiler_params=_params(("parallel", "parallel", "arbitrary")),
        name="matmul_" + getattr(epi, "__name__", "epi").strip("_"),
    )(x, *([w] * nw), *tiles, *rows)
    return outs


def _epi_plain(accs, tiles, rows):
    return (accs[0],)


def _epi_dup(accs, tiles, rows):
    return (accs[0], accs[0])


def _epi_scaled(scale, accs, tiles, rows):
    return (accs[0] * scale,)


def _epi_residual(accs, tiles, rows):
    return (tiles[0] + accs[0],)


def _epi_swiglu(accs, tiles, rows):
    g, u = accs
    return (g * _sigmoid(g) * u,)


def _epi_glu_residual(accs, tiles, rows):
    za, zb = accs
    return (tiles[0] + za * _sigmoid(zb),)


def _epi_log_sigmoid(accs, tiles, rows):
    z = accs[0] + rows[0]
    return (jnp.minimum(z, 0.0) - jnp.log1p(jnp.exp(-jnp.abs(z))),)


ROWS_MM_BLOCK_BYTES = 12 * 1024 * 1024


def _rows_mm_body(nk, has_res, x_ref, w_ref, *refs):
    res_ref = refs[0] if has_res else None
    y_ref, wb_ref = refs[-2:]
    wb = w_ref[...].astype(BF16)
    wb_ref[...] = wb
    part = jnp.dot(x_ref[...].astype(BF16), wb, preferred_element_type=F32)
    first = part + res_ref[...] if has_res else part
    if nk == 1:
        y_ref[...] = first
    else:
        k = pl.program_id(1)

        @pl.when(k == 0)
        def _():
            y_ref[...] = first

        @pl.when(k > 0)
        def _():
            y_ref[...] += part


def _rows_matmul(x, w3, layer, res=None, bn=512):
    m, kdim = x.shape
    n = w3.shape[2]
    assert w3.shape[1] == kdim and n % bn == 0
    units = kdim // LANES
    nk = next(d for d in range(1, units + 1) if units % d == 0 and (kdim // d) * bn * 4 <= ROWS_MM_BLOCK_BYTES)
    bk = kdim // nk
    in_specs = [pl.BlockSpec((m, bk), lambda j, k: (0, k)),
                pl.BlockSpec((None, bk, bn), lambda j, k: (layer, k, j))]
    operands = [x, w3]
    if res is not None:
        in_specs.append(pl.BlockSpec((m, bn), lambda j, k: (0, j)))
        operands.append(res)
    return pl.pallas_call(
        functools.partial(_rows_mm_body, nk, res is not None),
        grid=(n // bn, nk),
        in_specs=in_specs,
        out_specs=[pl.BlockSpec((m, bn), lambda j, k: (0, j)), pl.BlockSpec((bk, bn), lambda j, k: (k, j))],
        out_shape=[jax.ShapeDtypeStruct((m, n), F32), jax.ShapeDtypeStruct((kdim, n), BF16)],
        compiler_params=_params(("parallel", "arbitrary")),
        name="rows_matmul_cast",
    )(*operands)


def _rows_gated_body(swish_gate, has_res, z_ref, *refs):
    o_ref = refs[-1]
    n = o_ref.shape[1]
    a = z_ref[:, :n]
    b = z_ref[:, n:]
    out = a * _sigmoid(a) * b if swish_gate else a * _sigmoid(b)
    o_ref[...] = refs[0][...] + out if has_res else out


def _rows_gated(z, swish_gate, res=None):
    m, n2 = z.shape
    operands = [z] if res is None else [z, res]
    return pl.pallas_call(
        functools.partial(_rows_gated_body, swish_gate, res is not None),
        out_shape=jax.ShapeDtypeStruct((m, n2 // 2), F32),
        compiler_params=pltpu.CompilerParams(vmem_limit_bytes=V7X_VMEM_LIMIT),
        name="rows_gated",
    )(*operands)


def _cmul(ar, ai, br, bi):
    return ar * br - ai * bi, ar * bi + ai * br


def _s5_prep_body(t, c, n_sq, lre_ref, lim_ref, ldt_ref, btr_ref, bti_ref, cre_ref, cim_ref,
                  tz_ref, ws_ref, wy_ref, pws_ref, ab_ref, bb_ref):
    rows, p = lre_ref.shape
    sw = ws_ref.shape[-1]
    lam_re = lre_ref[...]
    lam_im = lim_ref[...]
    dt = jnp.exp(ldt_ref[...])
    mag = jnp.exp(lam_re * dt)
    ab_re = mag * jnp.cos(lam_im * dt)
    ab_im = mag * jnp.sin(lam_im * dt)
    den = lam_re * lam_re + lam_im * lam_im
    co_re = ((ab_re - 1.0) * lam_re + ab_im * lam_im) / den
    co_im = (ab_im * lam_re - (ab_re - 1.0) * lam_im) / den
    bb_re, bb_im = _cmul(co_re, co_im, btr_ref[...], bti_ref[...])
    c_re = cre_ref[...]
    c_im = cim_ref[...]
    ab_ref[0] = ab_re
    ab_ref[1] = ab_im
    bb_ref[0] = bb_re
    bb_ref[1] = bb_im

    pw = [(jnp.ones_like(ab_re), jnp.zeros_like(ab_im))]
    for _ in range(t):
        pw.append(_cmul(pw[-1][0], pw[-1][1], ab_re, ab_im))

    def iota(shape, dim):
        return lax.broadcasted_iota(jnp.int32, shape, dim)

    same_group = iota((rows, rows), 0) // c == iota((rows, rows), 1) // c
    row_in_col_group = iota((rows, sw), 0) // c == iota((rows, sw), 1) // (2 * p)
    first_row_of_group = jnp.logical_and(row_in_col_group, iota((rows, sw), 0) % c == 0)

    def dot_nt_hi(a, b):
        return lax.dot_general(a, b, NT_DIMS, precision=HIGHEST, preferred_element_type=F32)

    def spread(w_re, w_im):
        packed = jnp.concatenate([w_re, w_im], axis=1)
        return jnp.where(row_in_col_group, jnp.concatenate([packed] * (sw // (2 * p)), axis=1), 0.0)

    tz_ref[0] = jnp.zeros(tz_ref.shape[1:], tz_ref.dtype)
    for d in range(t):
        pr, pi = pw[d]
        m_re, m_im = _cmul(c_re, c_im, pr, pi)
        kd_t = jnp.where(same_group, dot_nt_hi(bb_re, m_re) - dot_nt_hi(bb_im, m_im), 0.0).astype(tz_ref.dtype)
        for tau in range(t - d):
            tz_ref[0, tau * rows:(tau + 1) * rows, (tau + d) * rows:(tau + d + 1) * rows] = kd_t
        w_re, w_im = _cmul(bb_re, bb_im, pr, pi)
        tau = t - 1 - d
        ws_ref[0, tau * rows:(tau + 1) * rows, :] = spread(w_re, w_im).astype(ws_ref.dtype)
        qr, qi = pw[d + 1]
        y_re, y_im = _cmul(c_re, c_im, qr, qi)
        wy_ref[0, :, d * rows:(d + 1) * rows] = spread(y_re, -y_im).T.astype(wy_ref.dtype)

    def group_row(w):
        return jnp.sum(jnp.where(first_row_of_group, spread(w, w), 0.0), axis=0, keepdims=True)

    qr, qi = pw[t]
    for k in range(n_sq):
        pws_ref[0, k:k + 1, :] = group_row(qr)
        pws_ref[0, n_sq + k:n_sq + k + 1, :] = group_row(qi)
        qr, qi = _cmul(qr, qi, qr, qi)


def _s5_prep(lam_re, lam_im, log_dt, b_re, b_im, c_re, c_im, n_sq):
    g, p = lam_re.shape
    c = c_re.shape[1]
    t = SSM_CHUNK
    rows = LANES
    gl = rows // c
    nt = g // gl
    sw = gl * 2 * p
    per_row = lambda a: jnp.repeat(a, c, axis=0)
    bt_re = jnp.swapaxes(b_re, 1, 2).reshape(g * c, p)
    bt_im = jnp.swapaxes(b_im, 1, 2).reshape(g * c, p)
    rp = pl.BlockSpec((rows, p), lambda n: (n, 0))
    out_shapes = [
        jax.ShapeDtypeStruct((nt, t * rows, t * rows), BF16),
        jax.ShapeDtypeStruct((nt, t * rows, sw), BF16),
        jax.ShapeDtypeStruct((nt, sw, t * rows), BF16),
        jax.ShapeDtypeStruct((nt, 2 * n_sq, sw), F32),
        jax.ShapeDtypeStruct((2, g * c, p), F32),
        jax.ShapeDtypeStruct((2, g * c, p), F32),
    ]
    out_specs = [
        pl.BlockSpec((1, t * rows, t * rows), lambda n: (n, 0, 0)),
        pl.BlockSpec((1, t * rows, sw), lambda n: (n, 0, 0)),
        pl.BlockSpec((1, sw, t * rows), lambda n: (n, 0, 0)),
        pl.BlockSpec((1, 2 * n_sq, sw), lambda n: (n, 0, 0)),
        pl.BlockSpec((2, rows, p), lambda n: (0, n, 0)),
        pl.BlockSpec((2, rows, p), lambda n: (0, n, 0)),
    ]
    tz, ws, wy, pws, ab_rows, bb_rows = pl.pallas_call(
        functools.partial(_s5_prep_body, t, c, n_sq),
        grid=(nt,),
        in_specs=[rp, rp, pl.BlockSpec((rows, 1), lambda n: (n, 0)), rp, rp, rp, rp],
        out_specs=out_specs,
        out_shape=out_shapes,
        compiler_params=_params(("parallel",)),
        name="s5_prep",
    )(per_row(lam_re), per_row(lam_im), per_row(log_dt.reshape(g, 1)), bt_re, bt_im,
      c_re.reshape(g * c, p), c_im.reshape(g * c, p))
    ab = ab_rows.reshape(2, g, c, p)[:, :, 0, :]
    bb = bb_rows.reshape(2, g, c, p)
    return tz, ws, wy, pws, ab, bb


def _gelu_tanh(y):
    return 0.5 * y * (1.0 + jnp.tanh(math.sqrt(2.0 / math.pi) * (y + 0.044715 * (y * y * y))))


def _s5_tile_body(t, p, n_sq, u_ref, tz_ref, ws_ref, wy_ref, pws_ref, x0_ref, d_ref, o_ref, fin_ref, yt_ref):
    l, lanes = u_ref.shape
    nc = l // t
    sw = ws_ref.shape[-1]
    ustk = jnp.concatenate(
        [u_ref[pl.ds(tau, nc, stride=t), :].astype(BF16) for tau in range(t)], axis=1)
    e = jnp.dot(ustk, ws_ref[0], preferred_element_type=F32)
    y_intra = jnp.dot(ustk, tz_ref[0], preferred_element_type=F32)
    row = lax.broadcasted_iota(jnp.int32, (nc, 2 * p), 0)
    lane = lax.broadcasted_iota(jnp.int32, (nc, 2 * p), 1)
    sign = jnp.where(lane < p, -1.0, 1.0)

    def cmul_packed(ar, ai, z):
        return ar * z + (ai * sign) * pltpu.roll(z, p, 1)

    s_in = []
    for gi in range(sw // (2 * p)):
        cols = slice(gi * 2 * p, (gi + 1) * 2 * p)
        ar = [pws_ref[0, k:k + 1, cols] for k in range(n_sq)]
        ai = [pws_ref[0, n_sq + k:n_sq + k + 1, cols] for k in range(n_sq)]
        x0 = x0_ref[0, 0, :, cols]
        z = e[:, cols]
        z = z + jnp.where(row == 0, cmul_packed(ar[0], ai[0], jnp.broadcast_to(x0, (nc, 2 * p))), 0.0)
        for k in range(n_sq):
            sh = 1 << k
            zs = jnp.where(row >= sh, pltpu.roll(z, sh, 0), 0.0)
            z = z + cmul_packed(ar[k], ai[k], zs)
        fin_ref[0, 0, :, cols] = z[nc - 1:nc, :]
        s_in.append(jnp.where(row == 0, x0, pltpu.roll(z, 1, 0)).astype(BF16))
    s_in = jnp.concatenate(s_in, axis=1)
    ystk = y_intra + jnp.dot(s_in, wy_ref[0], preferred_element_type=F32)
    for tau in range(t):
        yt_ref[pl.ds(tau, nc, stride=t), :] = ystk[:, tau * lanes:(tau + 1) * lanes]
    o_ref[...] = _gelu_tanh(yt_ref[...] + d_ref[...] * u_ref[...]).astype(o_ref.dtype)


def _s5_prompt(u, tz, ws, wy, pws, x0, d, nb, p):
    m, dm = u.shape
    l = m // nb
    nt, _, sw = ws.shape
    t = SSM_CHUNK
    n_sq = pws.shape[1] // 2
    assert dm == nt * LANES and l % (8 * t) == 0 and (1 << n_sq) >= l // t
    body = functools.partial(_s5_tile_body, t, p, n_sq)
    return pl.pallas_call(
        body,
        grid=(nt, nb),
        in_specs=[
            pl.BlockSpec((l, LANES), lambda n, b: (b, n)),
            pl.BlockSpec((1, t * LANES, t * LANES), lambda n, b: (n, 0, 0)),
            pl.BlockSpec((1, t * LANES, sw), lambda n, b: (n, 0, 0)),
            pl.BlockSpec((1, sw, t * LANES), lambda n, b: (n, 0, 0)),
            pl.BlockSpec((1, 2 * n_sq, sw), lambda n, b: (n, 0, 0)),
            pl.BlockSpec((1, 1, 1, sw), lambda n, b: (b, n, 0, 0)),
            pl.BlockSpec((1, LANES), lambda n, b: (0, n)),
        ],
        out_specs=[
            pl.BlockSpec((l, LANES), lambda n, b: (b, n)),
            pl.BlockSpec((1, 1, 1, sw), lambda n, b: (b, n, 0, 0)),
        ],
        out_shape=[jax.ShapeDtypeStruct((m, dm), BF16), jax.ShapeDtypeStruct((nb, nt, 1, sw), F32)],
        scratch_shapes=[pltpu.VMEM((l, LANES), F32)],
        compiler_params=_params(("parallel", "parallel")),
        name="s5_prompt",
    )(u, tz, ws, wy, pws, x0, d.reshape(1, dm))


def _s5_step_body(u_ref, x0r_ref, x0i_ref, abr_ref, abi_ref, wbr_ref, wbi_ref, wcr_ref, wci_ref, d_ref,
                  g_ref, sr_ref, si_ref):
    u = u_ref[...]
    ub = u.astype(BF16)
    bu_re = jnp.dot(ub, wbr_ref[0], preferred_element_type=F32)
    bu_im = jnp.dot(ub, wbi_ref[0], preferred_element_type=F32)
    ar, ai = abr_ref[...], abi_ref[...]
    s0r, s0i = _cmul(ar, ai, x0r_ref[...], x0i_ref[...])
    s_re = bu_re + s0r
    s_im = bu_im + s0i
    sr_ref[...] = s_re
    si_ref[...] = s_im
    y = (jnp.dot(s_re.astype(BF16), wcr_ref[0], preferred_element_type=F32)
         - jnp.dot(s_im.astype(BF16), wci_ref[0], preferred_element_type=F32)
         + d_ref[...] * u)
    g_ref[...] = _gelu_tanh(y)


def _s5_step(u, x0_re, x0_im, ab, bb, c_re, c_im, d):
    bd, dm = u.shape
    g, p = ab.shape[1:]
    c = dm // g
    gl = LANES // c
    nt = g // gl
    sl = gl * p
    eye = jnp.eye(gl, dtype=F32)

    def bdiag_in(w):
        w = w.reshape(nt, gl, c, p)
        return jnp.einsum("nacp,ab->nacbp", w, eye).reshape(nt, gl * c, gl * p).astype(BF16)

    def bdiag_out(w):
        w = w.reshape(nt, gl, c, p)
        return jnp.einsum("nacp,ab->napbc", w, eye).reshape(nt, gl * p, gl * c).astype(BF16)

    row = lambda a: a.reshape(1, g * p)
    lane_blk = pl.BlockSpec((bd, LANES), lambda n: (0, n))
    st_blk = pl.BlockSpec((bd, sl), lambda n: (0, n))
    st_row = pl.BlockSpec((1, sl), lambda n: (0, n))
    w_in_blk = pl.BlockSpec((1, LANES, sl), lambda n: (n, 0, 0))
    w_out_blk = pl.BlockSpec((1, sl, LANES), lambda n: (n, 0, 0))
    return pl.pallas_call(
        _s5_step_body,
        grid=(nt,),
        in_specs=[lane_blk, st_blk, st_blk, st_row, st_row, w_in_blk, w_in_blk, w_out_blk, w_out_blk,
                  pl.BlockSpec((1, LANES), lambda n: (0, n))],
        out_specs=[lane_blk, st_blk, st_blk],
        out_shape=[jax.ShapeDtypeStruct((bd, dm), F32), jax.ShapeDtypeStruct((bd, g * p), F32),
                   jax.ShapeDtypeStruct((bd, g * p), F32)],
        compiler_params=_params(("parallel",)),
        name="s5_step",
    )(u, x0_re.reshape(bd, g * p), x0_im.reshape(bd, g * p), row(ab[0]), row(ab[1]),
      bdiag_in(bb[0]), bdiag_in(bb[1]), bdiag_out(c_re), bdiag_out(c_im), d.reshape(1, dm))


def _cumsum_body(nblk, tb, scale, x_ref, o_ref):
    r = lax.broadcasted_iota(jnp.int32, (tb, tb), 0)
    c = lax.broadcasted_iota(jnp.int32, (tb, tb), 1)
    tri = jnp.where(c <= r, 1.0, 0.0).astype(F32)
    carry = jnp.zeros((1, x_ref.shape[-1]), F32)
    for i in range(nblk):
        blk = jnp.dot(tri, x_ref[0, i * tb:(i + 1) * tb, :], precision=HIGHEST, preferred_element_type=F32) + carry
        o_ref[0, i * tb:(i + 1) * tb, :] = blk * scale
        carry = blk[tb - 1:tb, :]


def _cumsum_seq(x, scale, tb=256):
    b, l, h = x.shape
    tb = min(tb, l)
    blk = pl.BlockSpec((1, l, h), lambda i: (i, 0, 0))
    return pl.pallas_call(
        functools.partial(_cumsum_body, l // tb, tb, scale),
        grid=(b,),
        in_specs=[blk],
        out_specs=blk,
        out_shape=jax.ShapeDtypeStruct((b, l, h), F32),
        compiler_params=_params(("parallel",)),
        name="logf_cumsum",
    )(x)


def _fox_prompt_body(blk, q_ref, k_ref, vt_ref, c_ref, ct_ref, o_ref, ck_ref, m_ref, l_ref, acc_ref):
    h = pl.program_id(1)
    i = pl.program_id(2)
    sub = blk // FOX_ROW_SPLIT

    @pl.when(i == 0)
    def _():
        c_all = c_ref[0]
        head = lax.broadcasted_iota(jnp.int32, c_all.shape, 1)
        ck_ref[...] = jnp.sum(jnp.where(head == h, c_all, 0.0), axis=-1, keepdims=True)

    m_ref[...] = jnp.full(m_ref.shape, -jnp.inf, F32)
    l_ref[...] = jnp.zeros(l_ref.shape, F32)
    acc_ref[...] = jnp.zeros(acc_ref.shape, F32)
    c_q_all = ct_ref[0, 0, pl.ds(i, 1), :]

    def kv_block(j, causal):
        start = pl.multiple_of(j * blk, blk)
        subs = range(FOX_ROW_SPLIT)
        cols = [slice(r * sub, (r + 1) * sub) for r in subs]
        n_keys = [(r + 1) * sub if causal else blk for r in subs]
        s = [lax.dot_general(k_ref[0, pl.ds(start, n_keys[r]), :], q_ref[0, cols[r], :], NT_DIMS,
                             preferred_element_type=F32) for r in subs]
        pr, alpha = [], []
        for r in subs:
            c_q = c_q_all[:, cols[r]]
            s_r = s[r] - ck_ref[pl.ds(start, n_keys[r]), :]
            if causal:
                k_pos = lax.broadcasted_iota(jnp.int32, (n_keys[r], sub), 0)
                q_pos = r * sub + lax.broadcasted_iota(jnp.int32, (n_keys[r], sub), 1)
                s_r = jnp.where(k_pos <= q_pos, s_r, -jnp.inf)
            m_old = m_ref[:, cols[r]]
            m_new = jnp.maximum(m_old, jnp.max(s_r, axis=0, keepdims=True) + c_q)
            alpha.append(jnp.exp2(m_old - m_new))
            p_r = jnp.exp2(s_r - (m_new - c_q))
            l_ref[:, cols[r]] = alpha[r] * l_ref[:, cols[r]] + jnp.sum(p_r, axis=0, keepdims=True)
            m_ref[:, cols[r]] = m_new
            pr.append(p_r.astype(BF16))
        for r in subs:
            vt = vt_ref[0, 0, j][:, :n_keys[r]]
            acc_ref[:, cols[r]] = alpha[r] * acc_ref[:, cols[r]] + jnp.dot(vt, pr[r], preferred_element_type=F32)

    def past_block(j, carry):
        kv_block(j, False)
        return carry

    lax.fori_loop(0, i, past_block, 0)
    kv_block(i, True)
    o_ref[0] = (acc_ref[...] / l_ref[...]).T.astype(o_ref.dtype)


def _fox_prompt(q, k, v, c, n_heads, blk=512):
    b, l, d = q.shape
    dh = d // n_heads
    blk = min(blk, l)
    nblk = l // blk
    ct = jnp.swapaxes(c, 1, 2).reshape(b, n_heads, nblk, blk)
    vt = v.reshape(b, nblk, blk, n_heads, dh).transpose(0, 3, 1, 4, 2)
    return pl.pallas_call(
        functools.partial(_fox_prompt_body, blk),
        grid=(b, n_heads, nblk),
        in_specs=[
            pl.BlockSpec((1, blk, dh), lambda bi, h, i: (bi, i, h)),
            pl.BlockSpec((1, l, dh), lambda bi, h, i: (bi, 0, h)),
            pl.BlockSpec((1, 1, nblk, dh, blk), lambda bi, h, i: (bi, h, 0, 0, 0)),
            pl.BlockSpec((1, l, n_heads), lambda bi, h, i: (bi, 0, 0)),
            pl.BlockSpec((1, 1, nblk, blk), lambda bi, h, i: (bi, h, 0, 0)),
        ],
        out_specs=pl.BlockSpec((1, blk, dh), lambda bi, h, i: (bi, i, h)),
        out_shape=jax.ShapeDtypeStruct((b, l, d), BF16),
        scratch_shapes=[pltpu.VMEM((l, 1), F32), pltpu.VMEM((1, blk), F32), pltpu.VMEM((1, blk), F32),
                        pltpu.VMEM((dh, blk), F32)],
        compiler_params=_params(("parallel", "arbitrary", "arbitrary")),
        name="fox_prompt",
    )(q, k, vt, c, ct)


def _fox_decode_body(n_heads, dh, n_pp, pt_ref, q_ref, kn_ref, vn_ref, cn_ref, *refs):
    k_refs = refs[:n_pp]
    v_refs = refs[n_pp:2 * n_pp]
    lf_refs = refs[2 * n_pp:3 * n_pp]
    o_ref, m_ref, l_ref, acc_ref, carry_ref = refs[3 * n_pp:]
    step = pl.program_id(1)
    scale = dh ** -0.5
    q = q_ref[0].astype(BF16)

    @pl.when(step == 0)
    def _():
        kn = kn_ref[0].astype(BF16).astype(F32)
        m_ref[...] = jnp.sum(q.astype(F32) * kn, axis=-1, keepdims=True) * scale
        l_ref[...] = jnp.ones(l_ref.shape, F32)
        acc_ref[...] = vn_ref[0]
        carry_ref[...] = jnp.zeros(carry_ref.shape, F32)

    n_r, lanes = lf_refs[0].shape[1:]
    lane = lax.broadcasted_iota(jnp.int32, (n_r, lanes), 1)
    r_i = lax.broadcasted_iota(jnp.int32, (n_r, n_r), 0)
    c_i = lax.broadcasted_iota(jnp.int32, (n_r, n_r), 1)
    later_row = jnp.where(c_i > r_i, 1.0, 0.0).astype(F32)
    row_head = lax.broadcasted_iota(jnp.int32, (n_heads, lanes), 0)
    col_head = lax.broadcasted_iota(jnp.int32, (n_heads, lanes), 1) % n_heads
    own = row_head == col_head
    cn = cn_ref[0]

    for k_ref, v_ref, lf_ref in zip(k_refs, v_refs, lf_refs):
        lf = lf_ref[0]
        same_head_total = lf
        later_in_row = jnp.zeros_like(lf)
        for k in range(1, lanes // n_heads):
            same_head_total = same_head_total + pltpu.roll(lf, k * n_heads, 1)
            later_in_row = later_in_row + jnp.where(lane + k * n_heads < lanes,
                                                    pltpu.roll(lf, lanes - k * n_heads, 1), 0.0)
        later_rows = jnp.dot(later_row, same_head_total, precision=HIGHEST, preferred_element_type=F32)
        decay = later_in_row + later_rows + carry_ref[...]
        carry_ref[...] += jnp.sum(same_head_total, axis=0, keepdims=True)

        s = lax.dot_general(q, k_ref[0].astype(BF16), NT_DIMS, preferred_element_type=F32)
        s = jnp.concatenate(
            [jnp.where(own, s[:, r * lanes:(r + 1) * lanes] * scale + cn + decay[r:r + 1, :], -jnp.inf)
             for r in range(n_r)], axis=1)
        m_old = m_ref[...]
        m_new = jnp.maximum(m_old, jnp.max(s, axis=-1, keepdims=True))
        alpha = jnp.exp(m_old - m_new)
        pr = jnp.exp(s - m_new)
        l_ref[...] = alpha * l_ref[...] + jnp.sum(pr, axis=-1, keepdims=True)
        acc_ref[...] = alpha * acc_ref[...] + jnp.dot(pr.astype(BF16), v_ref[0].astype(BF16),
                                                      preferred_element_type=F32)
        m_ref[...] = m_new

    @pl.when(step == pl.num_programs(1) - 1)
    def _():
        o_ref[0] = acc_ref[...] / l_ref[...]


def _fox_decode(q, k_new, v_new, logf_new, cache_k, cache_v, cache_logf, page_table, n_pp=4):
    bd, n_heads, dh = q.shape
    n_pool, ps = cache_k.shape[:2]
    n_pages = page_table.shape[1]
    rows = ps * n_heads
    n_pp = math.gcd(n_pp, n_pages)
    assert LANES % n_heads == 0 and rows % LANES == 0
    tok = pl.BlockSpec((1, n_heads, dh), lambda b, s, pt: (b, 0, 0))

    def page_specs(block):
        return [pl.BlockSpec(block, functools.partial(
            lambda b, s, pt, i: (pt[b, n_pages - 1 - (s * n_pp + i)], 0, 0), i=i)) for i in range(n_pp)]

    grid_spec = pltpu.PrefetchScalarGridSpec(
        num_scalar_prefetch=1,
        grid=(bd, n_pages // n_pp),
        in_specs=[tok, tok, tok, pl.BlockSpec((1, n_heads, 1), lambda b, s, pt: (b, 0, 0)),
                  *page_specs((1, rows, dh)), *page_specs((1, rows, dh)),
                  *page_specs((1, rows // LANES, LANES))],
        out_specs=tok,
        scratch_shapes=[pltpu.VMEM((n_heads, 1), F32), pltpu.VMEM((n_heads, 1), F32),
                        pltpu.VMEM((n_heads, dh), F32), pltpu.VMEM((1, LANES), F32)],
    )
    return pl.pallas_call(
        functools.partial(_fox_decode_body, n_heads, dh, n_pp),
        grid_spec=grid_spec,
        out_shape=jax.ShapeDtypeStruct((bd, n_heads, dh), F32),
        compiler_params=_params(("parallel", "arbitrary")),
        name="fox_decode",
    )(page_table, q, k_new, v_new, logf_new.reshape(bd, n_heads, 1),
      *([cache_k.reshape(n_pool, rows, dh)] * n_pp), *([cache_v.reshape(n_pool, rows, dh)] * n_pp),
      *([cache_logf.reshape(n_pool, rows // LANES, LANES)] * n_pp))


def _ffn(h, g_norm, w_gate_up, w_down, d_ff, t):
    xn = _rmsnorm(h, g_norm, t["act"], t["bm_norm"])
    (a,) = _matmul(xn, w_gate_up, d_ff, (0, d_ff), _epi_swiglu, (t["act"],), bm=t["bm"], bn=t["bn_ff"])
    (h,) = _matmul(a, w_down, h.shape[1], (0,), _epi_residual, (F32,), bm=t["bm"], bn=t["bn_down"],
                   bk=t["bk_down"], tiles=(h,))
    return h


def _sample_trunk(x, p, s5, decode):
    m, dm = x.shape
    n_heads = p["kv_w_f"].shape[1]
    wb = {}

    def ffn(h, layer):
        xn = _rmsnorm(h, p["norm_ffn"][layer], F32, m)
        gu, w_gu = _rows_matmul(xn, p["ffn_w_gate_up"], layer)
        h, w_dn = _rows_matmul(_rows_gated(gu, True), p["ffn_w_down"], layer, res=h)
        wb.setdefault("ffn_w_gate_up", []).append(w_gu)
        wb.setdefault("ffn_w_down", []).append(w_dn)
        return h

    xn = _rmsnorm(x, p["norm_mix"][0], F32, m)
    u, wb["ssm_w_in"] = _rows_matmul(xn, p["ssm_w_in"], 0)
    g_act, state = s5(u)
    z, wb["ssm_w_glu"] = _rows_matmul(g_act, p["ssm_w_glu"], 0)
    h = _rows_gated(z, False, res=x)
    h = ffn(h, 0)
    z = _rmsnorm(h, p["norm_kv"], F32, m)
    k, wb["kv_w_k"] = _rows_matmul(z, p["kv_w_k"][None], 0)
    v, wb["kv_w_v"] = _rows_matmul(z, p["kv_w_v"][None], 0)
    wb["kv_w_f"] = p["kv_w_f"].astype(BF16)
    (logf,) = _matmul(z, wb["kv_w_f"], n_heads, (0,), _epi_log_sigmoid, (F32,), bm=m, bn=n_heads,
                      rows=(p["kv_b_f"].reshape(1, n_heads),))
    qn = _rmsnorm(h, p["norm_mix"][1], F32, m)
    q, wb["attn_w_q"] = _rows_matmul(qn, p["attn_w_q"], 0)
    att = decode(q, k, v, logf)
    h, wb["attn_w_o"] = _rows_matmul(att, p["attn_w_o"], 0, res=h)
    h = ffn(h, 1)
    y = _rmsnorm(h, p["norm_final"], F32, m)
    return (y, k, v, logf, state), wb


def _trunk_front(x, w, s5, t, q_scale):
    m, dm = x.shape
    xn = _rmsnorm(x, w["norm_mix"][0], t["act"], t["bm_norm"])
    (u,) = _matmul(xn, w["ssm_w_in"], dm, (0,), _epi_plain, (F32,), bm=t["bm"], bn=t["bn"])
    g_act, fin = s5(u)
    (h,) = _matmul(g_act, w["ssm_w_glu"], dm, (0, dm), _epi_glu_residual, (F32,), bm=t["bm"], bn=t["bn_glu"],
                   tiles=(x,))
    h = _ffn(h, w["norm_ffn"][0], w["ffn_w_gate_up"][0], w["ffn_w_down"][0], w["d_ff"], t)
    z = _rmsnorm(h, w["norm_kv"], t["act"], t["bm_norm"])
    k, kb = _matmul(z, w["kv_w_k"], dm, (0,), _epi_dup, (F32, t["act"]), bm=t["bm"], bn=t["bn"])
    v, vb = _matmul(z, w["kv_w_v"], dm, (0,), _epi_dup, (F32, t["act"]), bm=t["bm"], bn=t["bn"])
    n_heads = w["kv_w_f"].shape[1]
    (logf,) = _matmul(z, w["kv_w_f"], n_heads, (0,), _epi_log_sigmoid, (F32,), bm=t["bm"], bn=n_heads,
                      rows=(w["kv_b_f"].reshape(1, n_heads),))
    qn = _rmsnorm(h, w["norm_mix"][1], t["act"], t["bm_norm"])
    q_epi = _epi_plain if q_scale is None else functools.partial(_epi_scaled, q_scale)
    (q,) = _matmul(qn, w["attn_w_q"], dm, (0,), q_epi, (t["act"],), bm=t["bm"], bn=t["bn"])
    return h, q, k, v, kb, vb, logf, fin


def _trunk_back(h, att, w, t):
    (h,) = _matmul(att, w["attn_w_o"], h.shape[1], (0,), _epi_residual, (F32,), bm=t["bm"], bn=t["bn"], tiles=(h,))
    h = _ffn(h, w["norm_ffn"][1], w["ffn_w_gate_up"][1], w["ffn_w_down"][1], w["d_ff"], t)
    return _rmsnorm(h, w["norm_final"], F32, t["bm_norm"])


def _down_k_block(d_ff):
    units = d_ff // LANES
    for parts in range(2, units + 1):
        if units % parts == 0:
            return d_ff // parts
    return d_ff


def kernel(x_prompt, x_sample, state_ssm_re, state_ssm_im, cache_k, cache_v, cache_logf, page_table, norm_mix, norm_ffn, ssm_w_in, ssm_lambda_re, ssm_lambda_im, ssm_log_dt, ssm_b_re, ssm_b_im, ssm_c_re, ssm_c_im, ssm_d, ssm_w_glu, attn_w_q, attn_w_o, norm_kv, kv_w_k, kv_w_v, kv_w_f, kv_b_f, ffn_w_gate_up, ffn_w_down, norm_final):
    b, l, dm = x_prompt.shape
    bd, t_dec, _ = x_sample.shape
    assert t_dec == 1, "the sample group decodes one token per sequence"
    assert ssm_w_in.shape[0] == 1 and attn_w_q.shape[0] == 1, "one S5 layer followed by one FoX layer"
    g, p = ssm_lambda_re.shape[1:]
    n_heads = kv_w_f.shape[1]
    dh = dm // n_heads
    d_ff = ffn_w_down.shape[1]
    nc = l // SSM_CHUNK
    n_sq = max(1, (nc - 1).bit_length())

    params = dict(
        norm_mix=norm_mix, norm_ffn=norm_ffn, norm_kv=norm_kv, norm_final=norm_final, kv_b_f=kv_b_f, kv_w_f=kv_w_f,
        ssm_w_in=ssm_w_in, ssm_w_glu=ssm_w_glu, attn_w_q=attn_w_q, attn_w_o=attn_w_o, kv_w_k=kv_w_k, kv_w_v=kv_w_v,
        ffn_w_gate_up=ffn_w_gate_up, ffn_w_down=ffn_w_down,
    )
    tile_p = dict(act=BF16, bm=1024, bn=512, bn_glu=512, bn_ff=math.gcd(d_ff, 1024), bn_down=512,
                  bk_down=_down_k_block(d_ff), bm_norm=256)

    tz, ws, wy, pws, ab, bb = _s5_prep(
        ssm_lambda_re[0], ssm_lambda_im[0], ssm_log_dt[0], ssm_b_re[0], ssm_b_im[0], ssm_c_re[0], ssm_c_im[0], n_sq)
    nt, _, sw = ws.shape

    def s5_sample(u):
        g_act, s_re, s_im = _s5_step(u, state_ssm_re[0], state_ssm_im[0], ab, bb, ssm_c_re[0], ssm_c_im[0], ssm_d[0])
        return g_act, (s_re, s_im)

    def decode(q, k_new, v_new, logf_new):
        att = _fox_decode(q.reshape(bd, n_heads, dh), k_new.reshape(bd, n_heads, dh), v_new.reshape(bd, n_heads, dh),
                          logf_new, cache_k, cache_v, cache_logf, page_table)
        return att.reshape(bd, dm)

    (ys, ks, vs, logfs, (s_re, s_im)), w = _sample_trunk(x_sample.reshape(bd, dm), params, s5_sample, decode)
    w.update(norm_mix=norm_mix, norm_ffn=norm_ffn, norm_kv=norm_kv, norm_final=norm_final, kv_b_f=kv_b_f, d_ff=d_ff)
    y_sample = ys.reshape(bd, 1, dm)
    ssm_re_sample = s_re.reshape(1, bd, g, p)
    ssm_im_sample = s_im.reshape(1, bd, g, p)
    k_sample = ks.reshape(bd, 1, n_heads, dh)
    v_sample = vs.reshape(bd, 1, n_heads, dh)
    logf_sample = logfs.reshape(bd, 1, n_heads)

    def s5_prompt(u):
        x0 = jnp.zeros((b, nt, 1, sw), F32)
        return _s5_prompt(u, tz, ws, wy, pws, x0, ssm_d[0], b, p)

    xp = x_prompt.reshape(b * l, dm)
    h, q, k, v, kb, vb, logf, fin = _trunk_front(xp, w, s5_prompt, tile_p, dh ** -0.5 * LOG2E)
    cum = _cumsum_seq(logf.reshape(b, l, n_heads), LOG2E)
    att = _fox_prompt(q.reshape(b, l, dm), kb.reshape(b, l, dm), vb.reshape(b, l, dm), cum, n_heads)
    y_prompt = _trunk_back(h, att.reshape(b * l, dm), w, tile_p).reshape(b, l, dm)
    fin = fin.reshape(b, g, 2 * p)
    ssm_re_prompt = fin[None, :, :, :p]
    ssm_im_prompt = fin[None, :, :, p:]
    k_prompt = k.reshape(b, l, n_heads, dh)
    v_prompt = v.reshape(b, l, n_heads, dh)
    logf_prompt = logf.reshape(b, l, n_heads)

    return (y_prompt, y_sample, ssm_re_prompt, ssm_im_prompt, k_prompt, v_prompt, logf_prompt,
            ssm_re_sample, ssm_im_sample, k_sample, v_sample, logf_sample)
```

```python
import functools
import math

import jax
import jax.numpy as jnp
from jax import lax
from jax.experimental import pallas as pl
from jax.experimental.pallas import tpu as pltpu

F32 = jnp.float32
BF16 = jnp.bfloat16
RMS_EPS = 1e-6
LANES = 128
SSM_CHUNK = 8
FOX_ROW_SPLIT = 4
V7X_VMEM_LIMIT = 56 * 1024 * 1024
HIGHEST = lax.Precision.HIGHEST
LOG2E = math.log2(math.e)
NT_DIMS = (((1,), (1,)), ((), ()))


def _params(semantics, vmem_bytes=V7X_VMEM_LIMIT):
    return pltpu.CompilerParams(dimension_semantics=semantics, vmem_limit_bytes=vmem_bytes)


def _sigmoid(x):
    return 1.0 / (1.0 + jnp.exp(-x))


def _rmsnorm_body(n, x_ref, *refs):
    x = x_ref[...]
    y = x * lax.rsqrt(jnp.mean(x * x, axis=-1, keepdims=True) + RMS_EPS)
    for g_ref, o_ref in zip(refs[:n], refs[n:]):
        o_ref[...] = (y * g_ref[...]).astype(o_ref.dtype)


def _rmsnorm(x, gains, out_dtype, bm):
    m, d = x.shape
    bm = min(bm, m)
    n = len(gains)
    blk = pl.BlockSpec((bm, d), lambda i: (i, 0))
    return pl.pallas_call(
        functools.partial(_rmsnorm_body, n),
        grid=(m // bm,),
        in_specs=[blk] + [pl.BlockSpec((1, d), lambda i: (0, 0))] * n,
        out_specs=[blk] * n,
        out_shape=[jax.ShapeDtypeStruct((m, d), out_dtype)] * n,
        compiler_params=_params(("parallel",)),
        name="rmsnorm",
    )(x, *[g.reshape(1, d) for g in gains])


def _mm_body(nw, n_tile, n_row, n_prev, n_out, n_wb, nk, epi, *refs):
    pos = iter(range(len(refs)))
    take = lambda n: [refs[next(pos)] for _ in range(n)]
    (x_ref,), w_refs, t_refs, r_refs = take(1), take(nw), take(n_tile), take(n_row)
    take(n_prev)
    o_refs, wb_refs, acc_refs = take(n_out), take(n_wb), take(nw if nk > 1 else 0)
    x = x_ref[...].astype(BF16)
    ws = [w[...].astype(BF16) for w in w_refs]
    for wb_ref, wv in zip(wb_refs, ws):
        wb_ref[...] = wv
    parts = [jnp.dot(x, wv, preferred_element_type=F32) for wv in ws]

    def finish(accs):
        outs = epi(accs, [t[...] for t in t_refs], [r[...] for r in r_refs])
        for o, v in zip(o_refs, outs):
            o[...] = v.astype(o.dtype)

    if nk == 1:
        finish(parts)
    else:
        k = pl.program_id(2)

        @pl.when(k == 0)
        def _():
            for a, p in zip(acc_refs, parts):
                a[...] = p

        @pl.when(jnp.logical_and(k > 0, k < nk - 1))
        def _():
            for a, p in zip(acc_refs, parts):
                a[...] += p

        @pl.when(k == nk - 1)
        def _():
            finish([a[...] + p for a, p in zip(acc_refs, parts)])


def _matmul(x, weights, n_cols, epi, out_dtypes, *, bm, bn, bk=None, tiles=(), rows=(),
            row_blocks=None, prev_outs=(), cast_layer=None):
    m, kdim = x.shape
    bm = min(bm, m)
    bn = min(bn, n_cols)
    bk = kdim if bk is None else bk
    assert m % bm == 0 and n_cols % bn == 0 and kdim % bk == 0, (x.shape, bm, bn, bk)
    assert all(off % bn == 0 for _, off in weights)
    i0, n_i = (0, m // bm) if row_blocks is None else row_blocks
    nj, nk = n_cols // bn, kdim // bk
    nw = len(weights)
    casting = cast_layer is not None
    in_specs = [pl.BlockSpec((bm, bk), lambda i, j, k: (i + i0, k))]
    for _, off in weights:
        if casting:
            in_specs.append(pl.BlockSpec(
                (None, bk, bn), functools.partial(lambda i, j, k, o: (cast_layer, k, j + o), o=off // bn)))
        else:
            in_specs.append(pl.BlockSpec((bk, bn), functools.partial(lambda i, j, k, o: (k, j + o), o=off // bn)))
    out_blk = pl.BlockSpec((bm, bn), lambda i, j, k: (i + i0, j))
    in_specs += [out_blk for _ in tiles]
    in_specs += [pl.BlockSpec((1, bn), lambda i, j, k: (0, j)) for _ in rows]
    in_specs += [pl.BlockSpec(memory_space=pl.ANY) for _ in prev_outs]
    n_in = 1 + nw + len(tiles) + len(rows)
    out_specs = [out_blk for _ in out_dtypes]
    out_shape = [jax.ShapeDtypeStruct((m, n_cols), dt) for dt in out_dtypes]
    if casting:
        assert n_i == 1, "each weight block must be written exactly once"
        out_specs += [pl.BlockSpec((bk, bn), lambda i, j, k: (k, j)) for _ in weights]
        out_shape += [jax.ShapeDtypeStruct((kdim, n_cols), BF16) for _ in weights]
    body = functools.partial(_mm_body, nw, len(tiles), len(rows), len(prev_outs), len(out_dtypes),
                             nw if casting else 0, nk, epi)
    outs = pl.pallas_call(
        body,
        grid=(n_i, nj, nk),
        in_specs=in_specs,
        out_specs=out_specs,
        out_shape=out_shape,
        scratch_shapes=[pltpu.VMEM((bm, bn), F32) for _ in range(nw if nk > 1 else 0)],
        input_output_aliases={n_in + a: a for a in range(len(prev_outs))},
        compiler_params=_params(("parallel", "parallel", "arbitrary")),
        name="matmul_" + getattr(epi, "__name__", "epi").strip("_") + ("_cast" if casting else ""),
    )(x, *[w for w, _ in weights], *tiles, *rows, *prev_outs)
    if casting:
        return outs[:len(out_dtypes)], outs[len(out_dtypes):]
    return outs


def _matmul_cast_first(x, weights, layer, n_cols, epi, out_dtypes, bn_first, **kw):
    m = x.shape[0]
    bm = min(kw["bm"], m)
    outs, copies = _matmul(x, weights, n_cols, epi, out_dtypes, row_blocks=(0, 1), cast_layer=layer,
                           **{**kw, "bn": min(bn_first, kw["bn"])})
    if m > bm:
        outs = _matmul(x, [(wb, 0) for wb in copies], n_cols, epi, out_dtypes, row_blocks=(1, m // bm - 1),
                       prev_outs=outs, **kw)
    return outs, copies


def _epi_plain(accs, tiles, rows):
    return (accs[0],)


def _epi_kv(accs, tiles, rows):
    k, v = accs
    return (k, k, v, v)


def _epi_scaled(scale, accs, tiles, rows):
    return (accs[0] * scale,)


def _epi_residual(accs, tiles, rows):
    return (tiles[0] + accs[0],)


def _epi_swiglu(accs, tiles, rows):
    g, u = accs
    return (g * _sigmoid(g) * u,)


def _epi_glu_residual(accs, tiles, rows):
    za, zb = accs
    return (tiles[0] + za * _sigmoid(zb),)


def _epi_log_sigmoid(accs, tiles, rows):
    z = accs[0] + rows[0]
    return (jnp.minimum(z, 0.0) - jnp.log1p(jnp.exp(-jnp.abs(z))),)


def _cmul(ar, ai, br, bi):
    return ar * br - ai * bi, ar * bi + ai * br


def _s5_prep_body(t, c, n_sq, lre_ref, lim_ref, ldt_ref, btr_ref, bti_ref, cre_ref, cim_ref,
                  tz_ref, ws_ref, wy_ref, pws_ref, ab_ref, bb_ref):
    rows, p = lre_ref.shape
    sw = ws_ref.shape[-1]
    lam_re = lre_ref[...]
    lam_im = lim_ref[...]
    dt = jnp.exp(ldt_ref[...])
    mag = jnp.exp(lam_re * dt)
    ab_re = mag * jnp.cos(lam_im * dt)
    ab_im = mag * jnp.sin(lam_im * dt)
    den = lam_re * lam_re + lam_im * lam_im
    co_re = ((ab_re - 1.0) * lam_re + ab_im * lam_im) / den
    co_im = (ab_im * lam_re - (ab_re - 1.0) * lam_im) / den
    bb_re, bb_im = _cmul(co_re, co_im, btr_ref[...], bti_ref[...])
    c_re = cre_ref[...]
    c_im = cim_ref[...]
    ab_ref[0] = ab_re
    ab_ref[1] = ab_im
    bb_ref[0] = bb_re
    bb_ref[1] = bb_im

    pw = [(jnp.ones_like(ab_re), jnp.zeros_like(ab_im))]
    for _ in range(t):
        pw.append(_cmul(pw[-1][0], pw[-1][1], ab_re, ab_im))

    def iota(shape, dim):
        return lax.broadcasted_iota(jnp.int32, shape, dim)

    same_group = iota((rows, rows), 0) // c == iota((rows, rows), 1) // c
    row_in_col_group = iota((rows, sw), 0) // c == iota((rows, sw), 1) // (2 * p)
    first_row_of_group = jnp.logical_and(row_in_col_group, iota((rows, sw), 0) % c == 0)

    def dot_nt_hi(a, b):
        return lax.dot_general(a, b, NT_DIMS, precision=HIGHEST, preferred_element_type=F32)

    def spread(w_re, w_im):
        packed = jnp.concatenate([w_re, w_im], axis=1)
        return jnp.where(row_in_col_group, jnp.concatenate([packed] * (sw // (2 * p)), axis=1), 0.0)

    tz_ref[0] = jnp.zeros(tz_ref.shape[1:], tz_ref.dtype)
    for d in range(t):
        pr, pi = pw[d]
        m_re, m_im = _cmul(c_re, c_im, pr, pi)
        kd_t = jnp.where(same_group, dot_nt_hi(bb_re, m_re) - dot_nt_hi(bb_im, m_im), 0.0).astype(tz_ref.dtype)
        for tau in range(t - d):
            tz_ref[0, tau * rows:(tau + 1) * rows, (tau + d) * rows:(tau + d + 1) * rows] = kd_t
        w_re, w_im = _cmul(bb_re, bb_im, pr, pi)
        tau = t - 1 - d
        ws_ref[0, tau * rows:(tau + 1) * rows, :] = spread(w_re, w_im).astype(ws_ref.dtype)
        qr, qi = pw[d + 1]
        y_re, y_im = _cmul(c_re, c_im, qr, qi)
        wy_ref[0, :, d * rows:(d + 1) * rows] = spread(y_re, -y_im).T.astype(wy_ref.dtype)

    def group_row(w):
        return jnp.sum(jnp.where(first_row_of_group, spread(w, w), 0.0), axis=0, keepdims=True)

    qr, qi = pw[t]
    for k in range(n_sq):
        pws_ref[0, k:k + 1, :] = group_row(qr)
        pws_ref[0, n_sq + k:n_sq + k + 1, :] = group_row(qi)
        qr, qi = _cmul(qr, qi, qr, qi)


def _s5_prep(lam_re, lam_im, log_dt, b_re, b_im, c_re, c_im, n_sq):
    g, p = lam_re.shape
    c = c_re.shape[1]
    t = SSM_CHUNK
    rows = LANES
    gl = rows // c
    nt = g // gl
    sw = gl * 2 * p
    per_row = lambda a: jnp.repeat(a, c, axis=0)
    bt_re = jnp.swapaxes(b_re, 1, 2).reshape(g * c, p)
    bt_im = jnp.swapaxes(b_im, 1, 2).reshape(g * c, p)
    rp = pl.BlockSpec((rows, p), lambda n: (n, 0))
    out_shapes = [
        jax.ShapeDtypeStruct((nt, t * rows, t * rows), BF16),
        jax.ShapeDtypeStruct((nt, t * rows, sw), BF16),
        jax.ShapeDtypeStruct((nt, sw, t * rows), BF16),
        jax.ShapeDtypeStruct((nt, 2 * n_sq, sw), F32),
        jax.ShapeDtypeStruct((2, g * c, p), F32),
        jax.ShapeDtypeStruct((2, g * c, p), F32),
    ]
    out_specs = [
        pl.BlockSpec((1, t * rows, t * rows), lambda n: (n, 0, 0)),
        pl.BlockSpec((1, t * rows, sw), lambda n: (n, 0, 0)),
        pl.BlockSpec((1, sw, t * rows), lambda n: (n, 0, 0)),
        pl.BlockSpec((1, 2 * n_sq, sw), lambda n: (n, 0, 0)),
        pl.BlockSpec((2, rows, p), lambda n: (0, n, 0)),
        pl.BlockSpec((2, rows, p), lambda n: (0, n, 0)),
    ]
    tz, ws, wy, pws, ab_rows, bb_rows = pl.pallas_call(
        functools.partial(_s5_prep_body, t, c, n_sq),
        grid=(nt,),
        in_specs=[rp, rp, pl.BlockSpec((rows, 1), lambda n: (n, 0)), rp, rp, rp, rp],
        out_specs=out_specs,
        out_shape=out_shapes,
        compiler_params=_params(("parallel",)),
        name="s5_prep",
    )(per_row(lam_re), per_row(lam_im), per_row(log_dt.reshape(g, 1)), bt_re, bt_im,
      c_re.reshape(g * c, p), c_im.reshape(g * c, p))
    ab = ab_rows.reshape(2, g, c, p)[:, :, 0, :]
    bb = bb_rows.reshape(2, g, c, p)
    return tz, ws, wy, pws, ab, bb


def _gelu_tanh(y):
    return 0.5 * y * (1.0 + jnp.tanh(math.sqrt(2.0 / math.pi) * (y + 0.044715 * (y * y * y))))


def _s5_tile_body(t, p, n_sq, u_ref, tz_ref, ws_ref, wy_ref, pws_ref, x0_ref, d_ref, o_ref, fin_ref, yt_ref):
    l, lanes = u_ref.shape
    nc = l // t
    sw = ws_ref.shape[-1]
    ustk = jnp.concatenate(
        [u_ref[pl.ds(tau, nc, stride=t), :].astype(BF16) for tau in range(t)], axis=1)
    e = jnp.dot(ustk, ws_ref[0], preferred_element_type=F32)
    y_intra = jnp.dot(ustk, tz_ref[0], preferred_element_type=F32)
    row = lax.broadcasted_iota(jnp.int32, (nc, 2 * p), 0)
    lane = lax.broadcasted_iota(jnp.int32, (nc, 2 * p), 1)
    sign = jnp.where(lane < p, -1.0, 1.0)

    def cmul_packed(ar, ai, z):
        return ar * z + (ai * sign) * pltpu.roll(z, p, 1)

    s_in = []
    for gi in range(sw // (2 * p)):
        cols = slice(gi * 2 * p, (gi + 1) * 2 * p)
        ar = [pws_ref[0, k:k + 1, cols] for k in range(n_sq)]
        ai = [pws_ref[0, n_sq + k:n_sq + k + 1, cols] for k in range(n_sq)]
        x0 = x0_ref[0, 0, :, cols]
        z = e[:, cols]
        z = z + jnp.where(row == 0, cmul_packed(ar[0], ai[0], jnp.broadcast_to(x0, (nc, 2 * p))), 0.0)
        for k in range(n_sq):
            sh = 1 << k
            zs = jnp.where(row >= sh, pltpu.roll(z, sh, 0), 0.0)
            z = z + cmul_packed(ar[k], ai[k], zs)
        fin_ref[0, 0, :, cols] = z[nc - 1:nc, :]
        s_in.append(jnp.where(row == 0, x0, pltpu.roll(z, 1, 0)).astype(BF16))
    s_in = jnp.concatenate(s_in, axis=1)
    ystk = y_intra + jnp.dot(s_in, wy_ref[0], preferred_element_type=F32)
    for tau in range(t):
        yt_ref[pl.ds(tau, nc, stride=t), :] = ystk[:, tau * lanes:(tau + 1) * lanes]
    o_ref[...] = _gelu_tanh(yt_ref[...] + d_ref[...] * u_ref[...]).astype(o_ref.dtype)


def _s5_prompt(u, tz, ws, wy, pws, x0, d, nb, p):
    m, dm = u.shape
    l = m // nb
    nt, _, sw = ws.shape
    t = SSM_CHUNK
    n_sq = pws.shape[1] // 2
    assert dm == nt * LANES and l % (8 * t) == 0 and (1 << n_sq) >= l // t
    body = functools.partial(_s5_tile_body, t, p, n_sq)
    return pl.pallas_call(
        body,
        grid=(nt, nb),
        in_specs=[
            pl.BlockSpec((l, LANES), lambda n, b: (b, n)),
            pl.BlockSpec((1, t * LANES, t * LANES), lambda n, b: (n, 0, 0)),
            pl.BlockSpec((1, t * LANES, sw), lambda n, b: (n, 0, 0)),
            pl.BlockSpec((1, sw, t * LANES), lambda n, b: (n, 0, 0)),
            pl.BlockSpec((1, 2 * n_sq, sw), lambda n, b: (n, 0, 0)),
            pl.BlockSpec((1, 1, 1, sw), lambda n, b: (b, n, 0, 0)),
            pl.BlockSpec((1, LANES), lambda n, b: (0, n)),
        ],
        out_specs=[
            pl.BlockSpec((l, LANES), lambda n, b: (b, n)),
            pl.BlockSpec((1, 1, 1, sw), lambda n, b: (b, n, 0, 0)),
        ],
        out_shape=[jax.ShapeDtypeStruct((m, dm), BF16), jax.ShapeDtypeStruct((nb, nt, 1, sw), F32)],
        scratch_shapes=[pltpu.VMEM((l, LANES), F32)],
        compiler_params=_params(("parallel", "parallel")),
        name="s5_prompt",
    )(u, tz, ws, wy, pws, x0, d.reshape(1, dm))


def _s5_step_body(u_ref, x0r_ref, x0i_ref, abr_ref, abi_ref, wbr_ref, wbi_ref, wcr_ref, wci_ref, d_ref,
                  g_ref, sr_ref, si_ref):
    u = u_ref[...]
    ub = u.astype(BF16)
    bu_re = jnp.dot(ub, wbr_ref[0], preferred_element_type=F32)
    bu_im = jnp.dot(ub, wbi_ref[0], preferred_element_type=F32)
    ar, ai = abr_ref[...], abi_ref[...]
    s0r, s0i = _cmul(ar, ai, x0r_ref[...], x0i_ref[...])
    s_re = bu_re + s0r
    s_im = bu_im + s0i
    sr_ref[...] = s_re
    si_ref[...] = s_im
    y = (jnp.dot(s_re.astype(BF16), wcr_ref[0], preferred_element_type=F32)
         - jnp.dot(s_im.astype(BF16), wci_ref[0], preferred_element_type=F32)
         + d_ref[...] * u)
    g_ref[...] = _gelu_tanh(y)


def _s5_step(u, x0_re, x0_im, ab, bb, c_re, c_im, d):
    bd, dm = u.shape
    g, p = ab.shape[1:]
    c = dm // g
    gl = LANES // c
    nt = g // gl
    sl = gl * p
    eye = jnp.eye(gl, dtype=F32)

    def bdiag_in(w):
        w = w.reshape(nt, gl, c, p)
        return jnp.einsum("nacp,ab->nacbp", w, eye).reshape(nt, gl * c, gl * p).astype(BF16)

    def bdiag_out(w):
        w = w.reshape(nt, gl, c, p)
        return jnp.einsum("nacp,ab->napbc", w, eye).reshape(nt, gl * p, gl * c).astype(BF16)

    row = lambda a: a.reshape(1, g * p)
    lane_blk = pl.BlockSpec((bd, LANES), lambda n: (0, n))
    st_blk = pl.BlockSpec((bd, sl), lambda n: (0, n))
    st_row = pl.BlockSpec((1, sl), lambda n: (0, n))
    w_in_blk = pl.BlockSpec((1, LANES, sl), lambda n: (n, 0, 0))
    w_out_blk = pl.BlockSpec((1, sl, LANES), lambda n: (n, 0, 0))
    return pl.pallas_call(
        _s5_step_body,
        grid=(nt,),
        in_specs=[lane_blk, st_blk, st_blk, st_row, st_row, w_in_blk, w_in_blk, w_out_blk, w_out_blk,
                  pl.BlockSpec((1, LANES), lambda n: (0, n))],
        out_specs=[lane_blk, st_blk, st_blk],
        out_shape=[jax.ShapeDtypeStruct((bd, dm), F32), jax.ShapeDtypeStruct((bd, g * p), F32),
                   jax.ShapeDtypeStruct((bd, g * p), F32)],
        compiler_params=_params(("parallel",)),
        name="s5_step",
    )(u, x0_re.reshape(bd, g * p), x0_im.reshape(bd, g * p), row(ab[0]), row(ab[1]),
      bdiag_in(bb[0]), bdiag_in(bb[1]), bdiag_out(c_re), bdiag_out(c_im), d.reshape(1, dm))


def _cumsum_body(nblk, tb, scale, x_ref, o_ref):
    r = lax.broadcasted_iota(jnp.int32, (tb, tb), 0)
    c = lax.broadcasted_iota(jnp.int32, (tb, tb), 1)
    tri = jnp.where(c <= r, 1.0, 0.0).astype(F32)
    carry = jnp.zeros((1, x_ref.shape[-1]), F32)
    for i in range(nblk):
        blk = jnp.dot(tri, x_ref[0, i * tb:(i + 1) * tb, :], precision=HIGHEST, preferred_element_type=F32) + carry
        o_ref[0, i * tb:(i + 1) * tb, :] = blk * scale
        carry = blk[tb - 1:tb, :]


def _cumsum_seq(x, scale, tb=256):
    b, l, h = x.shape
    tb = min(tb, l)
    blk = pl.BlockSpec((1, l, h), lambda i: (i, 0, 0))
    return pl.pallas_call(
        functools.partial(_cumsum_body, l // tb, tb, scale),
        grid=(b,),
        in_specs=[blk],
        out_specs=blk,
        out_shape=jax.ShapeDtypeStruct((b, l, h), F32),
        compiler_params=_params(("parallel",)),
        name="logf_cumsum",
    )(x)


def _fox_prompt_body(blk, q_ref, k_ref, vt_ref, c_ref, ct_ref, o_ref, ck_ref, m_ref, l_ref, acc_ref):
    h = pl.program_id(1)
    i = pl.program_id(2)
    sub = blk // FOX_ROW_SPLIT

    @pl.when(i == 0)
    def _():
        c_all = c_ref[0]
        head = lax.broadcasted_iota(jnp.int32, c_all.shape, 1)
        ck_ref[...] = jnp.sum(jnp.where(head == h, c_all, 0.0), axis=-1, keepdims=True)

    m_ref[...] = jnp.full(m_ref.shape, -jnp.inf, F32)
    l_ref[...] = jnp.zeros(l_ref.shape, F32)
    acc_ref[...] = jnp.zeros(acc_ref.shape, F32)
    c_q_all = ct_ref[0, 0, pl.ds(i, 1), :]

    def kv_block(j, causal):
        start = pl.multiple_of(j * blk, blk)
        subs = range(FOX_ROW_SPLIT)
        cols = [slice(r * sub, (r + 1) * sub) for r in subs]
        n_keys = [(r + 1) * sub if causal else blk for r in subs]
        s = [lax.dot_general(k_ref[0, pl.ds(start, n_keys[r]), :], q_ref[0, cols[r], :], NT_DIMS,
                             preferred_element_type=F32) for r in subs]
        pr, alpha = [], []
        for r in subs:
            c_q = c_q_all[:, cols[r]]
            s_r = s[r] - ck_ref[pl.ds(start, n_keys[r]), :]
            if causal:
                k_pos = lax.broadcasted_iota(jnp.int32, (n_keys[r], sub), 0)
                q_pos = r * sub + lax.broadcasted_iota(jnp.int32, (n_keys[r], sub), 1)
                s_r = jnp.where(k_pos <= q_pos, s_r, -jnp.inf)
            m_old = m_ref[:, cols[r]]
            m_new = jnp.maximum(m_old, jnp.max(s_r, axis=0, keepdims=True) + c_q)
            alpha.append(jnp.exp2(m_old - m_new))
            p_r = jnp.exp2(s_r - (m_new - c_q))
            l_ref[:, cols[r]] = alpha[r] * l_ref[:, cols[r]] + jnp.sum(p_r, axis=0, keepdims=True)
            m_ref[:, cols[r]] = m_new
            pr.append(p_r.astype(BF16))
        for r in subs:
            vt = vt_ref[0, 0, j][:, :n_keys[r]]
            acc_ref[:, cols[r]] = alpha[r] * acc_ref[:, cols[r]] + jnp.dot(vt, pr[r], preferred_element_type=F32)

    def past_block(j, carry):
        kv_block(j, False)
        return carry

    lax.fori_loop(0, i, past_block, 0)
    kv_block(i, True)
    o_ref[0] = (acc_ref[...] / l_ref[...]).T.astype(o_ref.dtype)


def _fox_prompt(q, k, v, c, n_heads, blk=512):
    b, l, d = q.shape
    dh = d // n_heads
    blk = min(blk, l)
    nblk = l // blk
    ct = jnp.swapaxes(c, 1, 2).reshape(b, n_heads, nblk, blk)
    vt = v.reshape(b, nblk, blk, n_heads, dh).transpose(0, 3, 1, 4, 2)
    return pl.pallas_call(
        functools.partial(_fox_prompt_body, blk),
        grid=(b, n_heads, nblk),
        in_specs=[
            pl.BlockSpec((1, blk, dh), lambda bi, h, i: (bi, i, h)),
            pl.BlockSpec((1, l, dh), lambda bi, h, i: (bi, 0, h)),
            pl.BlockSpec((1, 1, nblk, dh, blk), lambda bi, h, i: (bi, h, 0, 0, 0)),
            pl.BlockSpec((1, l, n_heads), lambda bi, h, i: (bi, 0, 0)),
            pl.BlockSpec((1, 1, nblk, blk), lambda bi, h, i: (bi, h, 0, 0)),
        ],
        out_specs=pl.BlockSpec((1, blk, dh), lambda bi, h, i: (bi, i, h)),
        out_shape=jax.ShapeDtypeStruct((b, l, d), BF16),
        scratch_shapes=[pltpu.VMEM((l, 1), F32), pltpu.VMEM((1, blk), F32), pltpu.VMEM((1, blk), F32),
                        pltpu.VMEM((dh, blk), F32)],
        compiler_params=_params(("parallel", "arbitrary", "arbitrary")),
        name="fox_prompt",
    )(q, k, vt, c, ct)


def _fox_decode_body(n_heads, dh, n_pp, pt_ref, q_ref, kn_ref, vn_ref, cn_ref, *refs):
    k_refs = refs[:n_pp]
    v_refs = refs[n_pp:2 * n_pp]
    lf_refs = refs[2 * n_pp:3 * n_pp]
    o_ref, m_ref, l_ref, acc_ref, carry_ref = refs[3 * n_pp:]
    step = pl.program_id(1)
    scale = dh ** -0.5
    q = q_ref[0].astype(BF16)

    @pl.when(step == 0)
    def _():
        kn = kn_ref[0].astype(BF16).astype(F32)
        m_ref[...] = jnp.sum(q.astype(F32) * kn, axis=-1, keepdims=True) * scale
        l_ref[...] = jnp.ones(l_ref.shape, F32)
        acc_ref[...] = vn_ref[0]
        carry_ref[...] = jnp.zeros(carry_ref.shape, F32)

    n_r, lanes = lf_refs[0].shape[1:]
    lane = lax.broadcasted_iota(jnp.int32, (n_r, lanes), 1)
    r_i = lax.broadcasted_iota(jnp.int32, (n_r, n_r), 0)
    c_i = lax.broadcasted_iota(jnp.int32, (n_r, n_r), 1)
    later_row = jnp.where(c_i > r_i, 1.0, 0.0).astype(F32)
    row_head = lax.broadcasted_iota(jnp.int32, (n_heads, lanes), 0)
    col_head = lax.broadcasted_iota(jnp.int32, (n_heads, lanes), 1) % n_heads
    own = row_head == col_head
    cn = cn_ref[0]

    for k_ref, v_ref, lf_ref in zip(k_refs, v_refs, lf_refs):
        lf = lf_ref[0]
        same_head_total = lf
        later_in_row = jnp.zeros_like(lf)
        for k in range(1, lanes // n_heads):
            same_head_total = same_head_total + pltpu.roll(lf, k * n_heads, 1)
            later_in_row = later_in_row + jnp.where(lane + k * n_heads < lanes,
                                                    pltpu.roll(lf, lanes - k * n_heads, 1), 0.0)
        later_rows = jnp.dot(later_row, same_head_total, precision=HIGHEST, preferred_element_type=F32)
        decay = later_in_row + later_rows + carry_ref[...]
        carry_ref[...] += jnp.sum(same_head_total, axis=0, keepdims=True)

        s = lax.dot_general(q, k_ref[0].astype(BF16), NT_DIMS, preferred_element_type=F32)
        s = jnp.concatenate(
            [jnp.where(own, s[:, r * lanes:(r + 1) * lanes] * scale + cn + decay[r:r + 1, :], -jnp.inf)
             for r in range(n_r)], axis=1)
        m_old = m_ref[...]
        m_new = jnp.maximum(m_old, jnp.max(s, axis=-1, keepdims=True))
        alpha = jnp.exp(m_old - m_new)
        pr = jnp.exp(s - m_new)
        l_ref[...] = alpha * l_ref[...] + jnp.sum(pr, axis=-1, keepdims=True)
        acc_ref[...] = alpha * acc_ref[...] + jnp.dot(pr.astype(BF16), v_ref[0].astype(BF16),
                                                      preferred_element_type=F32)
        m_ref[...] = m_new

    @pl.when(step == pl.num_programs(1) - 1)
    def _():
        o_ref[0] = acc_ref[...] / l_ref[...]


def _fox_decode(q, k_new, v_new, logf_new, cache_k, cache_v, cache_logf, page_table, n_pp=4):
    bd, n_heads, dh = q.shape
    n_pool, ps = cache_k.shape[:2]
    n_pages = page_table.shape[1]
    rows = ps * n_heads
    n_pp = math.gcd(n_pp, n_pages)
    assert LANES % n_heads == 0 and rows % LANES == 0
    tok = pl.BlockSpec((1, n_heads, dh), lambda b, s, pt: (b, 0, 0))

    def page_specs(block):
        return [pl.BlockSpec(block, functools.partial(
            lambda b, s, pt, i: (pt[b, n_pages - 1 - (s * n_pp + i)], 0, 0), i=i)) for i in range(n_pp)]

    grid_spec = pltpu.PrefetchScalarGridSpec(
        num_scalar_prefetch=1,
        grid=(bd, n_pages // n_pp),
        in_specs=[tok, tok, tok, pl.BlockSpec((1, n_heads, 1), lambda b, s, pt: (b, 0, 0)),
                  *page_specs((1, rows, dh)), *page_specs((1, rows, dh)),
                  *page_specs((1, rows // LANES, LANES))],
        out_specs=tok,
        scratch_shapes=[pltpu.VMEM((n_heads, 1), F32), pltpu.VMEM((n_heads, 1), F32),
                        pltpu.VMEM((n_heads, dh), F32), pltpu.VMEM((1, LANES), F32)],
    )
    return pl.pallas_call(
        functools.partial(_fox_decode_body, n_heads, dh, n_pp),
        grid_spec=grid_spec,
        out_shape=jax.ShapeDtypeStruct((bd, n_heads, dh), F32),
        compiler_params=_params(("parallel", "arbitrary")),
        name="fox_decode",
    )(page_table, q, k_new, v_new, logf_new.reshape(bd, n_heads, 1),
      *([cache_k.reshape(n_pool, rows, dh)] * n_pp), *([cache_v.reshape(n_pool, rows, dh)] * n_pp),
      *([cache_logf.reshape(n_pool, rows // LANES, LANES)] * n_pp))


class _Weights:
    def __init__(self, bn_first, **tensors):
        self.bn_first = bn_first
        self.t = tensors
        self.copies = {}

    def matmul(self, x, names, layer, offsets, n_cols, epi, out_dtypes, **kw):
        key = (tuple(names), layer)
        if key in self.copies:
            return _matmul(x, [(wb, 0) for wb in self.copies[key]], n_cols, epi, out_dtypes, **kw)
        stacked = [self.t[n] if self.t[n].ndim == 3 else self.t[n][None] for n in names]
        outs, self.copies[key] = _matmul_cast_first(x, list(zip(stacked, offsets)), layer, n_cols, epi, out_dtypes,
                                                    self.bn_first, **kw)
        return outs


def _ffn(h, w, layer, t):
    d_ff = w.t["ffn_w_down"].shape[1]
    (xn,) = _rmsnorm(h, [w.t["norm_ffn"][layer]], t["act"], t["bm_norm"])
    (a,) = w.matmul(xn, ["ffn_w_gate_up"] * 2, layer, (0, d_ff), d_ff, _epi_swiglu, (t["act"],),
                    bm=t["bm"], bn=t["bn_ff"])
    (h,) = w.matmul(a, ["ffn_w_down"], layer, (0,), h.shape[1], _epi_residual, (F32,), bm=t["bm"], bn=t["bn_down"],
                    bk=t["bk_down"], tiles=(h,))
    return h


def _trunk_front(x, w, s5, t, q_scale):
    m, dm = x.shape
    (xn,) = _rmsnorm(x, [w.t["norm_mix"][0]], t["act"], t["bm_norm"])
    (u,) = w.matmul(xn, ["ssm_w_in"], 0, (0,), dm, _epi_plain, (F32,), bm=t["bm"], bn=t["bn"])
    g_act, fin = s5(u)
    (h,) = w.matmul(g_act, ["ssm_w_glu"] * 2, 0, (0, dm), dm, _epi_glu_residual, (F32,), bm=t["bm"], bn=t["bn_glu"],
                    tiles=(x,))
    h = _ffn(h, w, 0, t)
    z, qn = _rmsnorm(h, [w.t["norm_kv"], w.t["norm_mix"][1]], t["act"], t["bm_norm"])
    k, kb, v, vb = w.matmul(z, ["kv_w_k", "kv_w_v"], 0, (0, 0), dm, _epi_kv, (F32, t["act"], F32, t["act"]),
                            bm=t["bm"], bn=t["bn_kv"])
    n_heads = w.t["kv_w_f"].shape[1]
    (logf,) = _matmul(z, [(w.t["kv_w_f"].astype(BF16), 0)], n_heads, _epi_log_sigmoid, (F32,), bm=t["bm"], bn=n_heads,
                      rows=(w.t["kv_b_f"].reshape(1, n_heads),))
    q_epi = _epi_plain if q_scale is None else functools.partial(_epi_scaled, q_scale)
    (q,) = w.matmul(qn, ["attn_w_q"], 0, (0,), dm, q_epi, (t["act"],), bm=t["bm"], bn=t["bn"])
    return h, q, k, v, kb, vb, logf, fin


def _trunk_back(h, att, w, t):
    (h,) = w.matmul(att, ["attn_w_o"], 0, (0,), h.shape[1], _epi_residual, (F32,), bm=t["bm"], bn=t["bn"], tiles=(h,))
    h = _ffn(h, w, 1, t)
    (y,) = _rmsnorm(h, [w.t["norm_final"]], F32, t["bm_norm"])
    return y


def _down_k_block(d_ff):
    units = d_ff // LANES
    for parts in range(2, units + 1):
        if units % parts == 0:
            return d_ff // parts
    return d_ff


def kernel(x_prompt, x_sample, state_ssm_re, state_ssm_im, cache_k, cache_v, cache_logf, page_table, norm_mix, norm_ffn, ssm_w_in, ssm_lambda_re, ssm_lambda_im, ssm_log_dt, ssm_b_re, ssm_b_im, ssm_c_re, ssm_c_im, ssm_d, ssm_w_glu, attn_w_q, attn_w_o, norm_kv, kv_w_k, kv_w_v, kv_w_f, kv_b_f, ffn_w_gate_up, ffn_w_down, norm_final):
    b, l, dm = x_prompt.shape
    bd, t_dec, _ = x_sample.shape
    assert t_dec == 1, "the sample group decodes one token per sequence"
    assert ssm_w_in.shape[0] == 1 and attn_w_q.shape[0] == 1, "one S5 layer followed by one FoX layer"
    g, p = ssm_lambda_re.shape[1:]
    n_heads = kv_w_f.shape[1]
    dh = dm // n_heads
    d_ff = ffn_w_down.shape[1]
    nc = l // SSM_CHUNK
    n_sq = max(1, (nc - 1).bit_length())

    w = _Weights(
        256,
        norm_mix=norm_mix, norm_ffn=norm_ffn, norm_kv=norm_kv, norm_final=norm_final, kv_b_f=kv_b_f, kv_w_f=kv_w_f,
        ssm_w_in=ssm_w_in, ssm_w_glu=ssm_w_glu, attn_w_q=attn_w_q, attn_w_o=attn_w_o, kv_w_k=kv_w_k, kv_w_v=kv_w_v,
        ffn_w_gate_up=ffn_w_gate_up, ffn_w_down=ffn_w_down,
    )
    bn_ff = math.gcd(d_ff, 1024)
    bk_down = _down_k_block(d_ff)
    tile_p = dict(act=BF16, bm=1024, bn=512, bn_kv=256, bn_glu=256, bn_ff=bn_ff, bn_down=512, bk_down=bk_down,
                  bm_norm=256)
    tile_s = dict(act=F32, bm=bd, bn=1024, bn_kv=512, bn_glu=512, bn_ff=bn_ff, bn_down=1024, bk_down=bk_down,
                  bm_norm=bd)

    tz, ws, wy, pws, ab, bb = _s5_prep(
        ssm_lambda_re[0], ssm_lambda_im[0], ssm_log_dt[0], ssm_b_re[0], ssm_b_im[0], ssm_c_re[0], ssm_c_im[0], n_sq)
    nt, _, sw = ws.shape

    def s5_prompt(u):
        x0 = jnp.zeros((b, nt, 1, sw), F32)
        return _s5_prompt(u, tz, ws, wy, pws, x0, ssm_d[0], b, p)

    xp = x_prompt.reshape(b * l, dm)
    h, q, k, v, kb, vb, logf, fin = _trunk_front(xp, w, s5_prompt, tile_p, dh ** -0.5 * LOG2E)
    cum = _cumsum_seq(logf.reshape(b, l, n_heads), LOG2E)
    att = _fox_prompt(q.reshape(b, l, dm), kb.reshape(b, l, dm), vb.reshape(b, l, dm), cum, n_heads)
    y_prompt = _trunk_back(h, att.reshape(b * l, dm), w, tile_p).reshape(b, l, dm)
    fin = fin.reshape(b, g, 2 * p)
    ssm_re_prompt = fin[None, :, :, :p]
    ssm_im_prompt = fin[None, :, :, p:]
    k_prompt = k.reshape(b, l, n_heads, dh)
    v_prompt = v.reshape(b, l, n_heads, dh)
    logf_prompt = logf.reshape(b, l, n_heads)

    def s5_sample(u):
        g_act, s_re, s_im = _s5_step(u, state_ssm_re[0], state_ssm_im[0], ab, bb, ssm_c_re[0], ssm_c_im[0], ssm_d[0])
        return g_act, (s_re, s_im)

    xs = x_sample.reshape(bd, dm)
    hs, qs, ks, vs, _, _, logfs, (s_re, s_im) = _trunk_front(xs, w, s5_sample, tile_s, None)
    att_s = _fox_decode(qs.reshape(bd, n_heads, dh), ks.reshape(bd, n_heads, dh), vs.reshape(bd, n_heads, dh),
                        logfs, cache_k, cache_v, cache_logf, page_table)
    y_sample = _trunk_back(hs, att_s.reshape(bd, dm), w, tile_s).reshape(bd, 1, dm)
    ssm_re_sample = s_re.reshape(1, bd, g, p)
    ssm_im_sample = s_im.reshape(1, bd, g, p)
    k_sample = ks.reshape(bd, 1, n_heads, dh)
    v_sample = vs.reshape(bd, 1, n_heads, dh)
    logf_sample = logfs.reshape(bd, 1, n_heads)

    return (y_prompt, y_sample, ssm_re_prompt, ssm_im_prompt, k_prompt, v_prompt, logf_prompt,
            ssm_re_sample, ssm_im_sample, k_sample, v_sample, logf_sample)
```

```python
import functools
import math

import jax
import jax.numpy as jnp
from jax import lax
from jax.experimental import pallas as pl
from jax.experimental.pallas import tpu as pltpu

F32 = jnp.float32
BF16 = jnp.bfloat16
RMS_EPS = 1e-6
LANES = 128
SSM_CHUNK = 8
FOX_ROW_SPLIT = 2
V7X_VMEM_LIMIT = 56 * 1024 * 1024
HIGHEST = lax.Precision.HIGHEST
LOG2E = math.log2(math.e)
NT_DIMS = (((1,), (1,)), ((), ()))


def _params(semantics, vmem_bytes=V7X_VMEM_LIMIT):
    return pltpu.CompilerParams(dimension_semantics=semantics, vmem_limit_bytes=vmem_bytes)


def _sigmoid(x):
    return 1.0 / (1.0 + jnp.exp(-x))


def _rmsnorm_body(n, x_ref, *refs):
    x = x_ref[...]
    y = x * lax.rsqrt(jnp.mean(x * x, axis=-1, keepdims=True) + RMS_EPS)
    for g_ref, o_ref in zip(refs[:n], refs[n:]):
        o_ref[...] = (y * g_ref[...]).astype(o_ref.dtype)


def _rmsnorm(x, gains, out_dtype, bm):
    m, d = x.shape
    bm = min(bm, m)
    n = len(gains)
    blk = pl.BlockSpec((bm, d), lambda i: (i, 0))
    return pl.pallas_call(
        functools.partial(_rmsnorm_body, n),
        grid=(m // bm,),
        in_specs=[blk] + [pl.BlockSpec((1, d), lambda i: (0, 0))] * n,
        out_specs=[blk] * n,
        out_shape=[jax.ShapeDtypeStruct((m, d), out_dtype)] * n,
        compiler_params=_params(("parallel",)),
        name="rmsnorm",
    )(x, *[g.reshape(1, d) for g in gains])


def _mm_body(nw, n_tile, n_row, n_prev, n_out, n_wb, nk, epi, *refs):
    pos = iter(range(len(refs)))
    take = lambda n: [refs[next(pos)] for _ in range(n)]
    (x_ref,), w_refs, t_refs, r_refs = take(1), take(nw), take(n_tile), take(n_row)
    take(n_prev)
    o_refs, wb_refs, acc_refs = take(n_out), take(n_wb), take(nw if nk > 1 else 0)
    x = x_ref[...].astype(BF16)
    ws = [w[...].astype(BF16) for w in w_refs]
    for wb_ref, wv in zip(wb_refs, ws):
        wb_ref[...] = wv
    parts = [jnp.dot(x, wv, preferred_element_type=F32) for wv in ws]

    def finish(accs):
        outs = epi(accs, [t[...] for t in t_refs], [r[...] for r in r_refs])
        for o, v in zip(o_refs, outs):
            o[...] = v.astype(o.dtype)

    if nk == 1:
        finish(parts)
    else:
        k = pl.program_id(2)

        @pl.when(k == 0)
        def _():
            for a, p in zip(acc_refs, parts):
                a[...] = p

        @pl.when(jnp.logical_and(k > 0, k < nk - 1))
        def _():
            for a, p in zip(acc_refs, parts):
                a[...] += p

        @pl.when(k == nk - 1)
        def _():
            finish([a[...] + p for a, p in zip(acc_refs, parts)])


def _matmul(x, weights, n_cols, epi, out_dtypes, *, bm, bn, bk=None, tiles=(), rows=(),
            row_blocks=None, prev_outs=(), cast_layer=None):
    m, kdim = x.shape
    bm = min(bm, m)
    bn = min(bn, n_cols)
    bk = kdim if bk is None else bk
    assert m % bm == 0 and n_cols % bn == 0 and kdim % bk == 0, (x.shape, bm, bn, bk)
    assert all(off % bn == 0 for _, off in weights)
    i0, n_i = (0, m // bm) if row_blocks is None else row_blocks
    nj, nk = n_cols // bn, kdim // bk
    nw = len(weights)
    casting = cast_layer is not None
    in_specs = [pl.BlockSpec((bm, bk), lambda i, j, k: (i + i0, k))]
    for _, off in weights:
        if casting:
            in_specs.append(pl.BlockSpec(
                (None, bk, bn), functools.partial(lambda i, j, k, o: (cast_layer, k, j + o), o=off // bn)))
        else:
            in_specs.append(pl.BlockSpec((bk, bn), functools.partial(lambda i, j, k, o: (k, j + o), o=off // bn)))
    out_blk = pl.BlockSpec((bm, bn), lambda i, j, k: (i + i0, j))
    in_specs += [out_blk for _ in tiles]
    in_specs += [pl.BlockSpec((1, bn), lambda i, j, k: (0, j)) for _ in rows]
    in_specs += [pl.BlockSpec(memory_space=pl.ANY) for _ in prev_outs]
    n_in = 1 + nw + len(tiles) + len(rows)
    out_specs = [out_blk for _ in out_dtypes]
    out_shape = [jax.ShapeDtypeStruct((m, n_cols), dt) for dt in out_dtypes]
    if casting:
        assert n_i == 1, "each weight block must be written exactly once"
        out_specs += [pl.BlockSpec((bk, bn), lambda i, j, k: (k, j)) for _ in weights]
        out_shape += [jax.ShapeDtypeStruct((kdim, n_cols), BF16) for _ in weights]
    body = functools.partial(_mm_body, nw, len(tiles), len(rows), len(prev_outs), len(out_dtypes),
                             nw if casting else 0, nk, epi)
    outs = pl.pallas_call(
        body,
        grid=(n_i, nj, nk),
        in_specs=in_specs,
        out_specs=out_specs,
        out_shape=out_shape,
        scratch_shapes=[pltpu.VMEM((bm, bn), F32) for _ in range(nw if nk > 1 else 0)],
        input_output_aliases={n_in + a: a for a in range(len(prev_outs))},
        compiler_params=_params(("parallel", "parallel", "arbitrary")),
        name="matmul_" + getattr(epi, "__name__", "epi").strip("_") + ("_cast" if casting else ""),
    )(x, *[w for w, _ in weights], *tiles, *rows, *prev_outs)
    if casting:
        return outs[:len(out_dtypes)], outs[len(out_dtypes):]
    return outs


def _matmul_cast_first(x, weights, layer, n_cols, epi, out_dtypes, bn_first, **kw):
    m = x.shape[0]
    bm = min(kw["bm"], m)
    outs, copies = _matmul(x, weights, n_cols, epi, out_dtypes, row_blocks=(0, 1), cast_layer=layer,
                           **{**kw, "bn": min(bn_first, kw["bn"])})
    if m > bm:
        outs = _matmul(x, [(wb, 0) for wb in copies], n_cols, epi, out_dtypes, row_blocks=(1, m // bm - 1),
                       prev_outs=outs, **kw)
    return outs, copies


def _epi_plain(accs, tiles, rows):
    return (accs[0],)


def _epi_kv(accs, tiles, rows):
    k, v = accs
    return (k, k, v, v)


def _epi_scaled(scale, accs, tiles, rows):
    return (accs[0] * scale,)


def _epi_residual(accs, tiles, rows):
    return (tiles[0] + accs[0],)


def _epi_swiglu(accs, tiles, rows):
    g, u = accs
    return (g * _sigmoid(g) * u,)


def _epi_glu_residual(accs, tiles, rows):
    za, zb = accs
    return (tiles[0] + za * _sigmoid(zb),)


def _epi_log_sigmoid(accs, tiles, rows):
    z = accs[0] + rows[0]
    return (jnp.minimum(z, 0.0) - jnp.log1p(jnp.exp(-jnp.abs(z))),)


def _cmul(ar, ai, br, bi):
    return ar * br - ai * bi, ar * bi + ai * br


def _s5_prep_body(t, c, n_sq, lre_ref, lim_ref, ldt_ref, btr_ref, bti_ref, cre_ref, cim_ref,
                  tz_ref, ws_ref, wy_ref, pws_ref, ab_ref, bb_ref):
    rows, p = lre_ref.shape
    sw = ws_ref.shape[-1]
    lam_re = lre_ref[...]
    lam_im = lim_ref[...]
    dt = jnp.exp(ldt_ref[...])
    mag = jnp.exp(lam_re * dt)
    ab_re = mag * jnp.cos(lam_im * dt)
    ab_im = mag * jnp.sin(lam_im * dt)
    den = lam_re * lam_re + lam_im * lam_im
    co_re = ((ab_re - 1.0) * lam_re + ab_im * lam_im) / den
    co_im = (ab_im * lam_re - (ab_re - 1.0) * lam_im) / den
    bb_re, bb_im = _cmul(co_re, co_im, btr_ref[...], bti_ref[...])
    c_re = cre_ref[...]
    c_im = cim_ref[...]
    ab_ref[0] = ab_re
    ab_ref[1] = ab_im
    bb_ref[0] = bb_re
    bb_ref[1] = bb_im

    pw = [(jnp.ones_like(ab_re), jnp.zeros_like(ab_im))]
    for _ in range(t):
        pw.append(_cmul(pw[-1][0], pw[-1][1], ab_re, ab_im))

    def iota(shape, dim):
        return lax.broadcasted_iota(jnp.int32, shape, dim)

    same_group = iota((rows, rows), 0) // c == iota((rows, rows), 1) // c
    row_in_col_group = iota((rows, sw), 0) // c == iota((rows, sw), 1) // (2 * p)
    first_row_of_group = jnp.logical_and(row_in_col_group, iota((rows, sw), 0) % c == 0)

    def dot_nt_hi(a, b):
        return lax.dot_general(a, b, NT_DIMS, precision=HIGHEST, preferred_element_type=F32)

    def spread(w_re, w_im):
        packed = jnp.concatenate([w_re, w_im], axis=1)
        return jnp.where(row_in_col_group, jnp.concatenate([packed] * (sw // (2 * p)), axis=1), 0.0)

    tz_ref[0] = jnp.zeros(tz_ref.shape[1:], tz_ref.dtype)
    for d in range(t):
        pr, pi = pw[d]
        m_re, m_im = _cmul(c_re, c_im, pr, pi)
        kd_t = jnp.where(same_group, dot_nt_hi(bb_re, m_re) - dot_nt_hi(bb_im, m_im), 0.0).astype(tz_ref.dtype)
        for tau in range(t - d):
            tz_ref[0, tau * rows:(tau + 1) * rows, (tau + d) * rows:(tau + d + 1) * rows] = kd_t
        w_re, w_im = _cmul(bb_re, bb_im, pr, pi)
        tau = t - 1 - d
        ws_ref[0, tau * rows:(tau + 1) * rows, :] = spread(w_re, w_im).astype(ws_ref.dtype)
        qr, qi = pw[d + 1]
        y_re, y_im = _cmul(c_re, c_im, qr, qi)
        wy_ref[0, :, d * rows:(d + 1) * rows] = spread(y_re, -y_im).T.astype(wy_ref.dtype)

    def group_row(w):
        return jnp.sum(jnp.where(first_row_of_group, spread(w, w), 0.0), axis=0, keepdims=True)

    qr, qi = pw[t]
    for k in range(n_sq):
        pws_ref[0, k:k + 1, :] = group_row(qr)
        pws_ref[0, n_sq + k:n_sq + k + 1, :] = group_row(qi)
        qr, qi = _cmul(qr, qi, qr, qi)


def _s5_prep(lam_re, lam_im, log_dt, b_re, b_im, c_re, c_im, n_sq):
    g, p = lam_re.shape
    c = c_re.shape[1]
    t = SSM_CHUNK
    rows = LANES
    gl = rows // c
    nt = g // gl
    sw = gl * 2 * p
    per_row = lambda a: jnp.repeat(a, c, axis=0)
    bt_re = jnp.swapaxes(b_re, 1, 2).reshape(g * c, p)
    bt_im = jnp.swapaxes(b_im, 1, 2).reshape(g * c, p)
    rp = pl.BlockSpec((rows, p), lambda n: (n, 0))
    out_shapes = [
        jax.ShapeDtypeStruct((nt, t * rows, t * rows), BF16),
        jax.ShapeDtypeStruct((nt, t * rows, sw), BF16),
        jax.ShapeDtypeStruct((nt, sw, t * rows), BF16),
        jax.ShapeDtypeStruct((nt, 2 * n_sq, sw), F32),
        jax.ShapeDtypeStruct((2, g * c, p), F32),
        jax.ShapeDtypeStruct((2, g * c, p), F32),
    ]
    out_specs = [
        pl.BlockSpec((1, t * rows, t * rows), lambda n: (n, 0, 0)),
        pl.BlockSpec((1, t * rows, sw), lambda n: (n, 0, 0)),
        pl.BlockSpec((1, sw, t * rows), lambda n: (n, 0, 0)),
        pl.BlockSpec((1, 2 * n_sq, sw), lambda n: (n, 0, 0)),
        pl.BlockSpec((2, rows, p), lambda n: (0, n, 0)),
        pl.BlockSpec((2, rows, p), lambda n: (0, n, 0)),
    ]
    tz, ws, wy, pws, ab_rows, bb_rows = pl.pallas_call(
        functools.partial(_s5_prep_body, t, c, n_sq),
        grid=(nt,),
        in_specs=[rp, rp, pl.BlockSpec((rows, 1), lambda n: (n, 0)), rp, rp, rp, rp],
        out_specs=out_specs,
        out_shape=out_shapes,
        compiler_params=_params(("parallel",)),
        name="s5_prep",
    )(per_row(lam_re), per_row(lam_im), per_row(log_dt.reshape(g, 1)), bt_re, bt_im,
      c_re.reshape(g * c, p), c_im.reshape(g * c, p))
    ab = ab_rows.reshape(2, g, c, p)[:, :, 0, :]
    bb = bb_rows.reshape(2, g, c, p)
    return tz, ws, wy, pws, ab, bb


def _gelu_tanh(y):
    return 0.5 * y * (1.0 + jnp.tanh(math.sqrt(2.0 / math.pi) * (y + 0.044715 * (y * y * y))))


def _s5_tile_body(t, p, n_sq, u_ref, tz_ref, ws_ref, wy_ref, pws_ref, x0_ref, d_ref, o_ref, fin_ref, yt_ref):
    l, lanes = u_ref.shape
    nc = l // t
    sw = ws_ref.shape[-1]
    ustk = jnp.concatenate(
        [u_ref[pl.ds(tau, nc, stride=t), :].astype(BF16) for tau in range(t)], axis=1)
    e = jnp.dot(ustk, ws_ref[0], preferred_element_type=F32)
    y_intra = jnp.dot(ustk, tz_ref[0], preferred_element_type=F32)
    row = lax.broadcasted_iota(jnp.int32, (nc, 2 * p), 0)
    lane = lax.broadcasted_iota(jnp.int32, (nc, 2 * p), 1)
    sign = jnp.where(lane < p, -1.0, 1.0)

    def cmul_packed(ar, ai, z):
        return ar * z + (ai * sign) * pltpu.roll(z, p, 1)

    s_in = []
    for gi in range(sw // (2 * p)):
        cols = slice(gi * 2 * p, (gi + 1) * 2 * p)
        ar = [pws_ref[0, k:k + 1, cols] for k in range(n_sq)]
        ai = [pws_ref[0, n_sq + k:n_sq + k + 1, cols] for k in range(n_sq)]
        x0 = x0_ref[0, 0, :, cols]
        z = e[:, cols]
        z = z + jnp.where(row == 0, cmul_packed(ar[0], ai[0], jnp.broadcast_to(x0, (nc, 2 * p))), 0.0)
        for k in range(n_sq):
            sh = 1 << k
            zs = jnp.where(row >= sh, pltpu.roll(z, sh, 0), 0.0)
            z = z + cmul_packed(ar[k], ai[k], zs)
        fin_ref[0, 0, :, cols] = z[nc - 1:nc, :]
        s_in.append(jnp.where(row == 0, x0, pltpu.roll(z, 1, 0)).astype(BF16))
    s_in = jnp.concatenate(s_in, axis=1)
    ystk = y_intra + jnp.dot(s_in, wy_ref[0], preferred_element_type=F32)
    for tau in range(t):
        yt_ref[pl.ds(tau, nc, stride=t), :] = ystk[:, tau * lanes:(tau + 1) * lanes]
    o_ref[...] = _gelu_tanh(yt_ref[...] + d_ref[...] * u_ref[...]).astype(o_ref.dtype)


def _s5_prompt(u, tz, ws, wy, pws, x0, d, nb, p):
    m, dm = u.shape
    l = m // nb
    nt, _, sw = ws.shape
    t = SSM_CHUNK
    n_sq = pws.shape[1] // 2
    assert dm == nt * LANES and l % (8 * t) == 0 and (1 << n_sq) >= l // t
    body = functools.partial(_s5_tile_body, t, p, n_sq)
    return pl.pallas_call(
        body,
        grid=(nt, nb),
        in_specs=[
            pl.BlockSpec((l, LANES), lambda n, b: (b, n)),
            pl.BlockSpec((1, t * LANES, t * LANES), lambda n, b: (n, 0, 0)),
            pl.BlockSpec((1, t * LANES, sw), lambda n, b: (n, 0, 0)),
            pl.BlockSpec((1, sw, t * LANES), lambda n, b: (n, 0, 0)),
            pl.BlockSpec((1, 2 * n_sq, sw), lambda n, b: (n, 0, 0)),
            pl.BlockSpec((1, 1, 1, sw), lambda n, b: (b, n, 0, 0)),
            pl.BlockSpec((1, LANES), lambda n, b: (0, n)),
        ],
        out_specs=[
            pl.BlockSpec((l, LANES), lambda n, b: (b, n)),
            pl.BlockSpec((1, 1, 1, sw), lambda n, b: (b, n, 0, 0)),
        ],
        out_shape=[jax.ShapeDtypeStruct((m, dm), BF16), jax.ShapeDtypeStruct((nb, nt, 1, sw), F32)],
        scratch_shapes=[pltpu.VMEM((l, LANES), F32)],
        compiler_params=_params(("parallel", "parallel")),
        name="s5_prompt",
    )(u, tz, ws, wy, pws, x0, d.reshape(1, dm))


def _s5_step_body(u_ref, x0r_ref, x0i_ref, abr_ref, abi_ref, wbr_ref, wbi_ref, wcr_ref, wci_ref, d_ref,
                  g_ref, sr_ref, si_ref):
    u = u_ref[...]
    ub = u.astype(BF16)
    bu_re = jnp.dot(ub, wbr_ref[0], preferred_element_type=F32)
    bu_im = jnp.dot(ub, wbi_ref[0], preferred_element_type=F32)
    ar, ai = abr_ref[...], abi_ref[...]
    s0r, s0i = _cmul(ar, ai, x0r_ref[...], x0i_ref[...])
    s_re = bu_re + s0r
    s_im = bu_im + s0i
    sr_ref[...] = s_re
    si_ref[...] = s_im
    y = (jnp.dot(s_re.astype(BF16), wcr_ref[0], preferred_element_type=F32)
         - jnp.dot(s_im.astype(BF16), wci_ref[0], preferred_element_type=F32)
         + d_ref[...] * u)
    g_ref[...] = _gelu_tanh(y)


def _s5_step(u, x0_re, x0_im, ab, bb, c_re, c_im, d):
    bd, dm = u.shape
    g, p = ab.shape[1:]
    c = dm // g
    gl = LANES // c
    nt = g // gl
    sl = gl * p
    eye = jnp.eye(gl, dtype=F32)

    def bdiag_in(w):
        w = w.reshape(nt, gl, c, p)
        return jnp.einsum("nacp,ab->nacbp", w, eye).reshape(nt, gl * c, gl * p).astype(BF16)

    def bdiag_out(w):
        w = w.reshape(nt, gl, c, p)
        return jnp.einsum("nacp,ab->napbc", w, eye).reshape(nt, gl * p, gl * c).astype(BF16)

    row = lambda a: a.reshape(1, g * p)
    lane_blk = pl.BlockSpec((bd, LANES), lambda n: (0, n))
    st_blk = pl.BlockSpec((bd, sl), lambda n: (0, n))
    st_row = pl.BlockSpec((1, sl), lambda n: (0, n))
    w_in_blk = pl.BlockSpec((1, LANES, sl), lambda n: (n, 0, 0))
    w_out_blk = pl.BlockSpec((1, sl, LANES), lambda n: (n, 0, 0))
    return pl.pallas_call(
        _s5_step_body,
        grid=(nt,),
        in_specs=[lane_blk, st_blk, st_blk, st_row, st_row, w_in_blk, w_in_blk, w_out_blk, w_out_blk,
                  pl.BlockSpec((1, LANES), lambda n: (0, n))],
        out_specs=[lane_blk, st_blk, st_blk],
        out_shape=[jax.ShapeDtypeStruct((bd, dm), F32), jax.ShapeDtypeStruct((bd, g * p), F32),
                   jax.ShapeDtypeStruct((bd, g * p), F32)],
        compiler_params=_params(("parallel",)),
        name="s5_step",
    )(u, x0_re.reshape(bd, g * p), x0_im.reshape(bd, g * p), row(ab[0]), row(ab[1]),
      bdiag_in(bb[0]), bdiag_in(bb[1]), bdiag_out(c_re), bdiag_out(c_im), d.reshape(1, dm))


def _cumsum_body(nblk, tb, scale, x_ref, o_ref):
    r = lax.broadcasted_iota(jnp.int32, (tb, tb), 0)
    c = lax.broadcasted_iota(jnp.int32, (tb, tb), 1)
    tri = jnp.where(c <= r, 1.0, 0.0).astype(F32)
    carry = jnp.zeros((1, x_ref.shape[-1]), F32)
    for i in range(nblk):
        blk = jnp.dot(tri, x_ref[0, i * tb:(i + 1) * tb, :], precision=HIGHEST, preferred_element_type=F32) + carry
        o_ref[0, i * tb:(i + 1) * tb, :] = blk * scale
        carry = blk[tb - 1:tb, :]


def _cumsum_seq(x, scale, tb=256):
    b, l, h = x.shape
    tb = min(tb, l)
    blk = pl.BlockSpec((1, l, h), lambda i: (i, 0, 0))
    return pl.pallas_call(
        functools.partial(_cumsum_body, l // tb, tb, scale),
        grid=(b,),
        in_specs=[blk],
        out_specs=blk,
        out_shape=jax.ShapeDtypeStruct((b, l, h), F32),
        compiler_params=_params(("parallel",)),
        name="logf_cumsum",
    )(x)


def _fox_prompt_body(blk, q_ref, k_ref, vt_ref, c_ref, ct_ref, o_ref, ck_ref, m_ref, l_ref, acc_ref, s_ref):
    h = pl.program_id(1)
    sub = blk // FOX_ROW_SPLIT
    n_tiles = ct_ref.shape[2]

    c_all = c_ref[0]
    head = lax.broadcasted_iota(jnp.int32, c_all.shape, 1)
    c_head = jnp.sum(jnp.where(head == h, c_all, 0.0), axis=-1, keepdims=True)
    ck_ref[...] = jnp.broadcast_to(c_head, ck_ref.shape)
    m_ref[...] = jnp.full(m_ref.shape, -jnp.inf, F32)
    l_ref[...] = jnp.zeros(l_ref.shape, F32)
    acc_ref[...] = jnp.zeros(acc_ref.shape, F32)

    def logits(n, i, j):
        s_ref[n % 2] = lax.dot_general(k_ref[0, j * blk:(j + 1) * blk, :], q_ref[0, i * blk:(i + 1) * blk, :],
                                       NT_DIMS, preferred_element_type=F32)

    def update(n, i, j):
        causal = i == j
        subs = range(FOX_ROW_SPLIT)
        cols = [slice(r * sub, (r + 1) * sub) for r in subs]
        n_keys = [(r + 1) * sub if causal else blk for r in subs]
        pr, alpha = [], []
        for r in subs:
            c_q = ct_ref[0, 0, i:i + 1, cols[r]]
            c_k = ck_ref[j * blk:j * blk + n_keys[r], :]
            s_r = s_ref[n % 2, :n_keys[r], cols[r]] - jnp.concatenate([c_k] * (sub // LANES), axis=1)
            if causal:
                k_pos = lax.broadcasted_iota(jnp.int32, (n_keys[r], sub), 0)
                q_pos = r * sub + lax.broadcasted_iota(jnp.int32, (n_keys[r], sub), 1)
                s_r = jnp.where(k_pos <= q_pos, s_r, -jnp.inf)
            m_old = m_ref[i, :, cols[r]]
            m_new = jnp.maximum(m_old, jnp.max(s_r, axis=0, keepdims=True) + c_q)
            alpha.append(jnp.exp2(m_old - m_new))
            p_r = jnp.exp2(s_r - (m_new - c_q))
            l_ref[i, :, cols[r]] = alpha[r] * l_ref[i, :, cols[r]] + jnp.sum(p_r, axis=0, keepdims=True)
            m_ref[i, :, cols[r]] = m_new
            pr.append(p_r.astype(BF16))
        for r in subs:
            vt = vt_ref[0, 0, j][:, :n_keys[r]]
            acc_ref[i, :, cols[r]] = (alpha[r] * acc_ref[i, :, cols[r]]
                                      + jnp.dot(vt, pr[r], preferred_element_type=F32))
        if causal:
            o_ref[0, i * blk:(i + 1) * blk, :] = (acc_ref[i] / l_ref[i]).T.astype(o_ref.dtype)

    pairs = [(i, j) for i in range(n_tiles) for j in range(i + 1)]
    logits(0, *pairs[0])
    for n, (i, j) in enumerate(pairs):
        if n + 1 < len(pairs):
            logits(n + 1, *pairs[n + 1])
        update(n, i, j)


def _fox_prompt(q, k, v, c, n_heads, blk=512):
    b, l, d = q.shape
    dh = d // n_heads
    blk = min(blk, l)
    nblk = l // blk
    ct = jnp.swapaxes(c, 1, 2).reshape(b, n_heads, nblk, blk)
    vt = v.reshape(b, nblk, blk, n_heads, dh).transpose(0, 3, 1, 4, 2)
    return pl.pallas_call(
        functools.partial(_fox_prompt_body, blk),
        grid=(b, n_heads),
        in_specs=[
            pl.BlockSpec((1, l, dh), lambda bi, h: (bi, 0, h)),
            pl.BlockSpec((1, l, dh), lambda bi, h: (bi, 0, h)),
            pl.BlockSpec((1, 1, nblk, dh, blk), lambda bi, h: (bi, h, 0, 0, 0)),
            pl.BlockSpec((1, l, n_heads), lambda bi, h: (bi, 0, 0)),
            pl.BlockSpec((1, 1, nblk, blk), lambda bi, h: (bi, h, 0, 0)),
        ],
        out_specs=pl.BlockSpec((1, l, dh), lambda bi, h: (bi, 0, h)),
        out_shape=jax.ShapeDtypeStruct((b, l, d), BF16),
        scratch_shapes=[pltpu.VMEM((l, LANES), F32), pltpu.VMEM((nblk, 1, blk), F32), pltpu.VMEM((nblk, 1, blk), F32),
                        pltpu.VMEM((nblk, dh, blk), F32), pltpu.VMEM((2, blk, blk), F32)],
        compiler_params=_params(("parallel", "parallel")),
        name="fox_prompt",
    )(q, k, vt, c, ct)


def _fox_decode_body(n_heads, dh, n_pp, pt_ref, q_ref, kn_ref, vn_ref, cn_ref, *refs):
    k_refs = refs[:n_pp]
    v_refs = refs[n_pp:2 * n_pp]
    lf_refs = refs[2 * n_pp:3 * n_pp]
    o_ref, m_ref, l_ref, acc_ref, carry_ref = refs[3 * n_pp:]
    step = pl.program_id(1)
    scale = dh ** -0.5
    q = q_ref[0].astype(BF16)

    @pl.when(step == 0)
    def _():
        kn = kn_ref[0].astype(BF16).astype(F32)
        m_ref[...] = jnp.sum(q.astype(F32) * kn, axis=-1, keepdims=True) * scale
        l_ref[...] = jnp.ones(l_ref.shape, F32)
        acc_ref[...] = vn_ref[0]
        carry_ref[...] = jnp.zeros(carry_ref.shape, F32)

    n_r, lanes = lf_refs[0].shape[1:]
    lane = lax.broadcasted_iota(jnp.int32, (n_r, lanes), 1)
    r_i = lax.broadcasted_iota(jnp.int32, (n_r, n_r), 0)
    c_i = lax.broadcasted_iota(jnp.int32, (n_r, n_r), 1)
    later_row = jnp.where(c_i > r_i, 1.0, 0.0).astype(F32)
    row_head = lax.broadcasted_iota(jnp.int32, (n_heads, lanes), 0)
    col_head = lax.broadcasted_iota(jnp.int32, (n_heads, lanes), 1) % n_heads
    own = row_head == col_head
    cn = cn_ref[0]

    for k_ref, v_ref, lf_ref in zip(k_refs, v_refs, lf_refs):
        lf = lf_ref[0]
        same_head_total = lf
        later_in_row = jnp.zeros_like(lf)
        for k in range(1, lanes // n_heads):
            same_head_total = same_head_total + pltpu.roll(lf, k * n_heads, 1)
            later_in_row = later_in_row + jnp.where(lane + k * n_heads < lanes,
                                                    pltpu.roll(lf, lanes - k * n_heads, 1), 0.0)
        later_rows = jnp.dot(later_row, same_head_total, precision=HIGHEST, preferred_element_type=F32)
        decay = later_in_row + later_rows + carry_ref[...]
        carry_ref[...] += jnp.sum(same_head_total, axis=0, keepdims=True)

        s = lax.dot_general(q, k_ref[0].astype(BF16), NT_DIMS, preferred_element_type=F32)
        s = jnp.concatenate(
            [jnp.where(own, s[:, r * lanes:(r + 1) * lanes] * scale + cn + decay[r:r + 1, :], -jnp.inf)
             for r in range(n_r)], axis=1)
        m_old = m_ref[...]
        m_new = jnp.maximum(m_old, jnp.max(s, axis=-1, keepdims=True))
        alpha = jnp.exp(m_old - m_new)
        pr = jnp.exp(s - m_new)
        l_ref[...] = alpha * l_ref[...] + jnp.sum(pr, axis=-1, keepdims=True)
        acc_ref[...] = alpha * acc_ref[...] + jnp.dot(pr.astype(BF16), v_ref[0].astype(BF16),
                                                      preferred_element_type=F32)
        m_ref[...] = m_new

    @pl.when(step == pl.num_programs(1) - 1)
    def _():
        o_ref[0] = acc_ref[...] / l_ref[...]


def _fox_decode(q, k_new, v_new, logf_new, cache_k, cache_v, cache_logf, page_table, n_pp=4):
    bd, n_heads, dh = q.shape
    n_pool, ps = cache_k.shape[:2]
    n_pages = page_table.shape[1]
    rows = ps * n_heads
    n_pp = math.gcd(n_pp, n_pages)
    assert LANES % n_heads == 0 and rows % LANES == 0
    tok = pl.BlockSpec((1, n_heads, dh), lambda b, s, pt: (b, 0, 0))

    def page_specs(block):
        return [pl.BlockSpec(block, functools.partial(
            lambda b, s, pt, i: (pt[b, n_pages - 1 - (s * n_pp + i)], 0, 0), i=i)) for i in range(n_pp)]

    grid_spec = pltpu.PrefetchScalarGridSpec(
        num_scalar_prefetch=1,
        grid=(bd, n_pages // n_pp),
        in_specs=[tok, tok, tok, pl.BlockSpec((1, n_heads, 1), lambda b, s, pt: (b, 0, 0)),
                  *page_specs((1, rows, dh)), *page_specs((1, rows, dh)),
                  *page_specs((1, rows // LANES, LANES))],
        out_specs=tok,
        scratch_shapes=[pltpu.VMEM((n_heads, 1), F32), pltpu.VMEM((n_heads, 1), F32),
                        pltpu.VMEM((n_heads, dh), F32), pltpu.VMEM((1, LANES), F32)],
    )
    return pl.pallas_call(
        functools.partial(_fox_decode_body, n_heads, dh, n_pp),
        grid_spec=grid_spec,
        out_shape=jax.ShapeDtypeStruct((bd, n_heads, dh), F32),
        compiler_params=_params(("parallel", "arbitrary")),
        name="fox_decode",
    )(page_table, q, k_new, v_new, logf_new.reshape(bd, n_heads, 1),
      *([cache_k.reshape(n_pool, rows, dh)] * n_pp), *([cache_v.reshape(n_pool, rows, dh)] * n_pp),
      *([cache_logf.reshape(n_pool, rows // LANES, LANES)] * n_pp))


class _Weights:
    def __init__(self, bn_first, **tensors):
        self.bn_first = bn_first
        self.t = tensors
        self.copies = {}

    def matmul(self, x, names, layer, offsets, n_cols, epi, out_dtypes, **kw):
        key = (tuple(names), layer)
        if key in self.copies:
            return _matmul(x, [(wb, 0) for wb in self.copies[key]], n_cols, epi, out_dtypes, **kw)
        stacked = [self.t[n] if self.t[n].ndim == 3 else self.t[n][None] for n in names]
        outs, self.copies[key] = _matmul_cast_first(x, list(zip(stacked, offsets)), layer, n_cols, epi, out_dtypes,
                                                    self.bn_first, **kw)
        return outs


def _ffn(h, w, layer, t):
    d_ff = w.t["ffn_w_down"].shape[1]
    (xn,) = _rmsnorm(h, [w.t["norm_ffn"][layer]], t["act"], t["bm_norm"])
    (a,) = w.matmul(xn, ["ffn_w_gate_up"] * 2, layer, (0, d_ff), d_ff, _epi_swiglu, (t["act"],),
                    bm=t["bm"], bn=t["bn_ff"])
    (h,) = w.matmul(a, ["ffn_w_down"], layer, (0,), h.shape[1], _epi_residual, (F32,), bm=t["bm"], bn=t["bn_down"],
                    bk=t["bk_down"], tiles=(h,))
    return h


def _trunk_front(x, w, s5, t, q_scale):
    m, dm = x.shape
    (xn,) = _rmsnorm(x, [w.t["norm_mix"][0]], t["act"], t["bm_norm"])
    (u,) = w.matmul(xn, ["ssm_w_in"], 0, (0,), dm, _epi_plain, (F32,), bm=t["bm"], bn=t["bn"])
    g_act, fin = s5(u)
    (h,) = w.matmul(g_act, ["ssm_w_glu"] * 2, 0, (0, dm), dm, _epi_glu_residual, (F32,), bm=t["bm"], bn=t["bn_glu"],
                    tiles=(x,))
    h = _ffn(h, w, 0, t)
    z, qn = _rmsnorm(h, [w.t["norm_kv"], w.t["norm_mix"][1]], t["act"], t["bm_norm"])
    k, kb, v, vb = w.matmul(z, ["kv_w_k", "kv_w_v"], 0, (0, 0), dm, _epi_kv, (F32, t["act"], F32, t["act"]),
                            bm=t["bm"], bn=t["bn_kv"])
    n_heads = w.t["kv_w_f"].shape[1]
    (logf,) = _matmul(z, [(w.t["kv_w_f"].astype(BF16), 0)], n_heads, _epi_log_sigmoid, (F32,), bm=t["bm"], bn=n_heads,
                      rows=(w.t["kv_b_f"].reshape(1, n_heads),))
    q_epi = _epi_plain if q_scale is None else functools.partial(_epi_scaled, q_scale)
    (q,) = w.matmul(qn, ["attn_w_q"], 0, (0,), dm, q_epi, (t["act"],), bm=t["bm"], bn=t["bn"])
    return h, q, k, v, kb, vb, logf, fin


def _trunk_back(h, att, w, t):
    (h,) = w.matmul(att, ["attn_w_o"], 0, (0,), h.shape[1], _epi_residual, (F32,), bm=t["bm"], bn=t["bn"], tiles=(h,))
    h = _ffn(h, w, 1, t)
    (y,) = _rmsnorm(h, [w.t["norm_final"]], F32, t["bm_norm"])
    return y


def _down_k_block(d_ff):
    units = d_ff // LANES
    for parts in range(2, units + 1):
        if units % parts == 0:
            return d_ff // parts
    return d_ff


def kernel(x_prompt, x_sample, state_ssm_re, state_ssm_im, cache_k, cache_v, cache_logf, page_table, norm_mix, norm_ffn, ssm_w_in, ssm_lambda_re, ssm_lambda_im, ssm_log_dt, ssm_b_re, ssm_b_im, ssm_c_re, ssm_c_im, ssm_d, ssm_w_glu, attn_w_q, attn_w_o, norm_kv, kv_w_k, kv_w_v, kv_w_f, kv_b_f, ffn_w_gate_up, ffn_w_down, norm_final):
    b, l, dm = x_prompt.shape
    bd, t_dec, _ = x_sample.shape
    assert t_dec == 1, "the sample group decodes one token per sequence"
    assert ssm_w_in.shape[0] == 1 and attn_w_q.shape[0] == 1, "one S5 layer followed by one FoX layer"
    g, p = ssm_lambda_re.shape[1:]
    n_heads = kv_w_f.shape[1]
    dh = dm // n_heads
    d_ff = ffn_w_down.shape[1]
    nc = l // SSM_CHUNK
    n_sq = max(1, (nc - 1).bit_length())

    w = _Weights(
        256,
        norm_mix=norm_mix, norm_ffn=norm_ffn, norm_kv=norm_kv, norm_final=norm_final, kv_b_f=kv_b_f, kv_w_f=kv_w_f,
        ssm_w_in=ssm_w_in, ssm_w_glu=ssm_w_glu, attn_w_q=attn_w_q, attn_w_o=attn_w_o, kv_w_k=kv_w_k, kv_w_v=kv_w_v,
        ffn_w_gate_up=ffn_w_gate_up, ffn_w_down=ffn_w_down,
    )
    bn_ff = math.gcd(d_ff, 1024)
    bk_down = _down_k_block(d_ff)
    tile_p = dict(act=BF16, bm=1024, bn=512, bn_kv=256, bn_glu=256, bn_ff=bn_ff, bn_down=512, bk_down=bk_down,
                  bm_norm=256)
    tile_s = dict(act=F32, bm=bd, bn=1024, bn_kv=512, bn_glu=512, bn_ff=bn_ff, bn_down=1024, bk_down=bk_down,
                  bm_norm=bd)

    tz, ws, wy, pws, ab, bb = _s5_prep(
        ssm_lambda_re[0], ssm_lambda_im[0], ssm_log_dt[0], ssm_b_re[0], ssm_b_im[0], ssm_c_re[0], ssm_c_im[0], n_sq)
    nt, _, sw = ws.shape

    def s5_prompt(u):
        x0 = jnp.zeros((b, nt, 1, sw), F32)
        return _s5_prompt(u, tz, ws, wy, pws, x0, ssm_d[0], b, p)

    xp = x_prompt.reshape(b * l, dm)
    h, q, k, v, kb, vb, logf, fin = _trunk_front(xp, w, s5_prompt, tile_p, dh ** -0.5 * LOG2E)
    cum = _cumsum_seq(logf.reshape(b, l, n_heads), LOG2E)
    att = _fox_prompt(q.reshape(b, l, dm), kb.reshape(b, l, dm), vb.reshape(b, l, dm), cum, n_heads)
    y_prompt = _trunk_back(h, att.reshape(b * l, dm), w, tile_p).reshape(b, l, dm)
    fin = fin.reshape(b, g, 2 * p)
    ssm_re_prompt = fin[None, :, :, :p]
    ssm_im_prompt = fin[None, :, :, p:]
    k_prompt = k.reshape(b, l, n_heads, dh)
    v_prompt = v.reshape(b, l, n_heads, dh)
    logf_prompt = logf.reshape(b, l, n_heads)

    def s5_sample(u):
        g_act, s_re, s_im = _s5_step(u, state_ssm_re[0], state_ssm_im[0], ab, bb, ssm_c_re[0], ssm_c_im[0], ssm_d[0])
        return g_act, (s_re, s_im)

    xs = x_sample.reshape(bd, dm)
    hs, qs, ks, vs, _, _, logfs, (s_re, s_im) = _trunk_front(xs, w, s5_sample, tile_s, None)
    att_s = _fox_decode(qs.reshape(bd, n_heads, dh), ks.reshape(bd, n_heads, dh), vs.reshape(bd, n_heads, dh),
                        logfs, cache_k, cache_v, cache_logf, page_table)
    y_sample = _trunk_back(hs, att_s.reshape(bd, dm), w, tile_s).reshape(bd, 1, dm)
    ssm_re_sample = s_re.reshape(1, bd, g, p)
    ssm_im_sample = s_im.reshape(1, bd, g, p)
    k_sample = ks.reshape(bd, 1, n_heads, dh)
    v_sample = vs.reshape(bd, 1, n_heads, dh)
    logf_sample = logfs.reshape(bd, 1, n_heads)

    return (y_prompt, y_sample, ssm_re_prompt, ssm_im_prompt, k_prompt, v_prompt, logf_prompt,
            ssm_re_sample, ssm_im_sample, k_sample, v_sample, logf_sample)
```

```python
import functools
import math

import jax
import jax.numpy as jnp
from jax import lax
from jax.experimental import pallas as pl
from jax.experimental.pallas import tpu as pltpu

F32 = jnp.float32
BF16 = jnp.bfloat16
RMS_EPS = 1e-6
LANES = 128
SSM_CHUNK = 8
FOX_ROW_SPLIT = 2
V7X_VMEM_LIMIT = 56 * 1024 * 1024
HIGHEST = lax.Precision.HIGHEST
LOG2E = math.log2(math.e)
NT_DIMS = (((1,), (1,)), ((), ()))


def _params(semantics, vmem_bytes=V7X_VMEM_LIMIT):
    return pltpu.CompilerParams(dimension_semantics=semantics, vmem_limit_bytes=vmem_bytes)


def _sigmoid(x):
    return 1.0 / (1.0 + jnp.exp(-x))


def _rmsnorm_body(n, x_ref, *refs):
    x = x_ref[...]
    y = x * lax.rsqrt(jnp.mean(x * x, axis=-1, keepdims=True) + RMS_EPS)
    for g_ref, o_ref in zip(refs[:n], refs[n:]):
        o_ref[...] = (y * g_ref[...]).astype(o_ref.dtype)


def _rmsnorm(x, gains, out_dtype, bm):
    m, d = x.shape
    bm = min(bm, m)
    n = len(gains)
    blk = pl.BlockSpec((bm, d), lambda i: (i, 0))
    return pl.pallas_call(
        functools.partial(_rmsnorm_body, n),
        grid=(m // bm,),
        in_specs=[blk] + [pl.BlockSpec((1, d), lambda i: (0, 0))] * n,
        out_specs=[blk] * n,
        out_shape=[jax.ShapeDtypeStruct((m, d), out_dtype)] * n,
        compiler_params=_params(("parallel",)),
        name="rmsnorm",
    )(x, *[g.reshape(1, d) for g in gains])


def _mm_body(nw, n_tile, n_row, n_prev, n_out, n_wb, nk, epi, *refs):
    pos = iter(range(len(refs)))
    take = lambda n: [refs[next(pos)] for _ in range(n)]
    (x_ref,), w_refs, t_refs, r_refs = take(1), take(nw), take(n_tile), take(n_row)
    take(n_prev)
    o_refs, wb_refs, acc_refs = take(n_out), take(n_wb), take(nw if nk > 1 else 0)
    x = x_ref[...].astype(BF16)
    ws = [w[...].astype(BF16) for w in w_refs]
    for wb_ref, wv in zip(wb_refs, ws):
        wb_ref[...] = wv
    parts = [jnp.dot(x, wv, preferred_element_type=F32) for wv in ws]

    def finish(accs):
        outs = epi(accs, [t[...] for t in t_refs], [r[...] for r in r_refs])
        for o, v in zip(o_refs, outs):
            o[...] = v.astype(o.dtype)

    if nk == 1:
        finish(parts)
    else:
        k = pl.program_id(2)

        @pl.when(k == 0)
        def _():
            for a, p in zip(acc_refs, parts):
                a[...] = p

        @pl.when(jnp.logical_and(k > 0, k < nk - 1))
        def _():
            for a, p in zip(acc_refs, parts):
                a[...] += p

        @pl.when(k == nk - 1)
        def _():
            finish([a[...] + p for a, p in zip(acc_refs, parts)])


def _matmul(x, weights, n_cols, epi, out_dtypes, *, bm, bn, bk=None, tiles=(), rows=(),
            row_blocks=None, prev_outs=(), cast_layer=None):
    m, kdim = x.shape
    bm = min(bm, m)
    bn = min(bn, n_cols)
    bk = kdim if bk is None else bk
    assert m % bm == 0 and n_cols % bn == 0 and kdim % bk == 0, (x.shape, bm, bn, bk)
    assert all(off % bn == 0 for _, off in weights)
    i0, n_i = (0, m // bm) if row_blocks is None else row_blocks
    nj, nk = n_cols // bn, kdim // bk
    nw = len(weights)
    casting = cast_layer is not None
    in_specs = [pl.BlockSpec((bm, bk), lambda i, j, k: (i + i0, k))]
    for _, off in weights:
        if casting:
            in_specs.append(pl.BlockSpec(
                (None, bk, bn), functools.partial(lambda i, j, k, o: (cast_layer, k, j + o), o=off // bn)))
        else:
            in_specs.append(pl.BlockSpec((bk, bn), functools.partial(lambda i, j, k, o: (k, j + o), o=off // bn)))
    out_blk = pl.BlockSpec((bm, bn), lambda i, j, k: (i + i0, j))
    in_specs += [out_blk for _ in tiles]
    in_specs += [pl.BlockSpec((1, bn), lambda i, j, k: (0, j)) for _ in rows]
    in_specs += [pl.BlockSpec(memory_space=pl.ANY) for _ in prev_outs]
    n_in = 1 + nw + len(tiles) + len(rows)
    out_specs = [out_blk for _ in out_dtypes]
    out_shape = [jax.ShapeDtypeStruct((m, n_cols), dt) for dt in out_dtypes]
    if casting:
        assert n_i == 1, "each weight block must be written exactly once"
        out_specs += [pl.BlockSpec((bk, bn), lambda i, j, k: (k, j)) for _ in weights]
        out_shape += [jax.ShapeDtypeStruct((kdim, n_cols), BF16) for _ in weights]
    body = functools.partial(_mm_body, nw, len(tiles), len(rows), len(prev_outs), len(out_dtypes),
                             nw if casting else 0, nk, epi)
    outs = pl.pallas_call(
        body,
        grid=(n_i, nj, nk),
        in_specs=in_specs,
        out_specs=out_specs,
        out_shape=out_shape,
        scratch_shapes=[pltpu.VMEM((bm, bn), F32) for _ in range(nw if nk > 1 else 0)],
        input_output_aliases={n_in + a: a for a in range(len(prev_outs))},
        compiler_params=_params(("parallel", "parallel", "arbitrary")),
        name="matmul_" + getattr(epi, "__name__", "epi").strip("_") + ("_cast" if casting else ""),
    )(x, *[w for w, _ in weights], *tiles, *rows, *prev_outs)
    if casting:
        return outs[:len(out_dtypes)], outs[len(out_dtypes):]
    return outs


def _matmul_cast_first(x, weights, layer, n_cols, epi, out_dtypes, bn_first, **kw):
    m = x.shape[0]
    bm = min(kw["bm"], m)
    outs, copies = _matmul(x, weights, n_cols, epi, out_dtypes, row_blocks=(0, 1), cast_layer=layer,
                           **{**kw, "bn": min(bn_first, kw["bn"])})
    if m > bm:
        outs = _matmul(x, [(wb, 0) for wb in copies], n_cols, epi, out_dtypes, row_blocks=(1, m // bm - 1),
                       prev_outs=outs, **kw)
    return outs, copies


def _epi_plain(accs, tiles, rows):
    return (accs[0],)


def _epi_kv(accs, tiles, rows):
    k, v = accs
    return (k, k, v, v)


def _epi_scaled(scale, accs, tiles, rows):
    return (accs[0] * scale,)


def _epi_residual(accs, tiles, rows):
    return (tiles[0] + accs[0],)


def _epi_swiglu(accs, tiles, rows):
    g, u = accs
    return (g * _sigmoid(g) * u,)


def _epi_glu_residual(accs, tiles, rows):
    za, zb = accs
    return (tiles[0] + za * _sigmoid(zb),)


def _epi_log_sigmoid(accs, tiles, rows):
    z = accs[0] + rows[0]
    return (jnp.minimum(z, 0.0) - jnp.log1p(jnp.exp(-jnp.abs(z))),)


def _cmul(ar, ai, br, bi):
    return ar * br - ai * bi, ar * bi + ai * br


def _s5_prep_body(t, c, n_sq, lre_ref, lim_ref, ldt_ref, btr_ref, bti_ref, cre_ref, cim_ref,
                  tz_ref, ws_ref, wy_ref, pws_ref, ab_ref, bb_ref):
    rows, p = lre_ref.shape
    sw = ws_ref.shape[-1]
    lam_re = lre_ref[...]
    lam_im = lim_ref[...]
    dt = jnp.exp(ldt_ref[...])
    mag = jnp.exp(lam_re * dt)
    ab_re = mag * jnp.cos(lam_im * dt)
    ab_im = mag * jnp.sin(lam_im * dt)
    den = lam_re * lam_re + lam_im * lam_im
    co_re = ((ab_re - 1.0) * lam_re + ab_im * lam_im) / den
    co_im = (ab_im * lam_re - (ab_re - 1.0) * lam_im) / den
    bb_re, bb_im = _cmul(co_re, co_im, btr_ref[...], bti_ref[...])
    c_re = cre_ref[...]
    c_im = cim_ref[...]
    ab_ref[0] = ab_re
    ab_ref[1] = ab_im
    bb_ref[0] = bb_re
    bb_ref[1] = bb_im

    pw = [(jnp.ones_like(ab_re), jnp.zeros_like(ab_im))]
    for _ in range(t):
        pw.append(_cmul(pw[-1][0], pw[-1][1], ab_re, ab_im))

    def iota(shape, dim):
        return lax.broadcasted_iota(jnp.int32, shape, dim)

    same_group = iota((rows, rows), 0) // c == iota((rows, rows), 1) // c
    gl = sw // (2 * p)
    row_in_col_group = iota((rows, sw), 0) // c == (iota((rows, sw), 1) % (gl * p)) // p
    first_row_of_group = jnp.logical_and(row_in_col_group, iota((rows, sw), 0) % c == 0)

    def dot_nt_hi(a, b):
        return lax.dot_general(a, b, NT_DIMS, precision=HIGHEST, preferred_element_type=F32)

    def spread(w_re, w_im):
        return jnp.where(row_in_col_group, jnp.concatenate([w_re] * gl + [w_im] * gl, axis=1), 0.0)

    tz_ref[0] = jnp.zeros(tz_ref.shape[1:], tz_ref.dtype)
    for d in range(t):
        pr, pi = pw[d]
        m_re, m_im = _cmul(c_re, c_im, pr, pi)
        kd_t = jnp.where(same_group, dot_nt_hi(bb_re, m_re) - dot_nt_hi(bb_im, m_im), 0.0).astype(tz_ref.dtype)
        for tau in range(t - d):
            tz_ref[0, tau * rows:(tau + 1) * rows, (tau + d) * rows:(tau + d + 1) * rows] = kd_t
        w_re, w_im = _cmul(bb_re, bb_im, pr, pi)
        tau = t - 1 - d
        ws_ref[0, tau * rows:(tau + 1) * rows, :] = spread(w_re, w_im).astype(ws_ref.dtype)
        qr, qi = pw[d + 1]
        y_re, y_im = _cmul(c_re, c_im, qr, qi)
        wy_ref[0, :, d * rows:(d + 1) * rows] = spread(y_re, -y_im).T.astype(wy_ref.dtype)

    qr, qi = pw[t]
    for k in range(n_sq):
        pws_ref[0, k:k + 1, :] = jnp.sum(jnp.where(first_row_of_group, spread(qr, qi), 0.0), axis=0, keepdims=True)
        qr, qi = _cmul(qr, qi, qr, qi)


def _s5_prep(lam_re, lam_im, log_dt, b_re, b_im, c_re, c_im, n_sq):
    g, p = lam_re.shape
    c = c_re.shape[1]
    t = SSM_CHUNK
    rows = LANES
    gl = rows // c
    nt = g // gl
    sw = gl * 2 * p
    per_row = lambda a: jnp.repeat(a, c, axis=0)
    bt_re = jnp.swapaxes(b_re, 1, 2).reshape(g * c, p)
    bt_im = jnp.swapaxes(b_im, 1, 2).reshape(g * c, p)
    rp = pl.BlockSpec((rows, p), lambda n: (n, 0))
    out_shapes = [
        jax.ShapeDtypeStruct((nt, t * rows, t * rows), BF16),
        jax.ShapeDtypeStruct((nt, t * rows, sw), BF16),
        jax.ShapeDtypeStruct((nt, sw, t * rows), BF16),
        jax.ShapeDtypeStruct((nt, n_sq, sw), F32),
        jax.ShapeDtypeStruct((2, g * c, p), F32),
        jax.ShapeDtypeStruct((2, g * c, p), F32),
    ]
    out_specs = [
        pl.BlockSpec((1, t * rows, t * rows), lambda n: (n, 0, 0)),
        pl.BlockSpec((1, t * rows, sw), lambda n: (n, 0, 0)),
        pl.BlockSpec((1, sw, t * rows), lambda n: (n, 0, 0)),
        pl.BlockSpec((1, n_sq, sw), lambda n: (n, 0, 0)),
        pl.BlockSpec((2, rows, p), lambda n: (0, n, 0)),
        pl.BlockSpec((2, rows, p), lambda n: (0, n, 0)),
    ]
    tz, ws, wy, pws, ab_rows, bb_rows = pl.pallas_call(
        functools.partial(_s5_prep_body, t, c, n_sq),
        grid=(nt,),
        in_specs=[rp, rp, pl.BlockSpec((rows, 1), lambda n: (n, 0)), rp, rp, rp, rp],
        out_specs=out_specs,
        out_shape=out_shapes,
        compiler_params=_params(("parallel",)),
        name="s5_prep",
    )(per_row(lam_re), per_row(lam_im), per_row(log_dt.reshape(g, 1)), bt_re, bt_im,
      c_re.reshape(g * c, p), c_im.reshape(g * c, p))
    ab = ab_rows.reshape(2, g, c, p)[:, :, 0, :]
    bb = bb_rows.reshape(2, g, c, p)
    return tz, ws, wy, pws, ab, bb


def _gelu_tanh(y):
    return 0.5 * y * (1.0 + jnp.tanh(math.sqrt(2.0 / math.pi) * (y + 0.044715 * (y * y * y))))


def _s5_tile_body(t, p, n_sq, u_ref, tz_ref, ws_ref, wy_ref, pws_ref, x0_ref, d_ref, o_ref, fin_ref, yt_ref):
    l, lanes = u_ref.shape
    nc = l // t
    sw = ws_ref.shape[-1]
    ustk = jnp.concatenate(
        [u_ref[pl.ds(tau, nc, stride=t), :].astype(BF16) for tau in range(t)], axis=1)
    e = jnp.dot(ustk, ws_ref[0], preferred_element_type=F32)
    y_intra = jnp.dot(ustk, tz_ref[0], preferred_element_type=F32)
    half = sw // 2
    row = lax.broadcasted_iota(jnp.int32, (nc, lanes), 0)

    def shift_rows(z, sh):
        if sh % 8 == 0:
            return jnp.concatenate([jnp.zeros((sh, lanes), F32), z[:nc - sh]], axis=0)
        return jnp.where(row >= sh, pltpu.roll(z, sh, 0), 0.0)

    s_re, s_im = [], []
    for ci in range(half // lanes):
        c_re = slice(ci * lanes, (ci + 1) * lanes)
        c_im = slice(half + ci * lanes, half + (ci + 1) * lanes)
        x0_re, x0_im = x0_ref[0, 0, :, c_re], x0_ref[0, 0, :, c_im]
        a0r, a0i = pws_ref[0, 0:1, c_re], pws_ref[0, 0:1, c_im]
        first = row == 0
        z_re = e[:, c_re] + jnp.where(first, a0r * x0_re - a0i * x0_im, 0.0)
        z_im = e[:, c_im] + jnp.where(first, a0r * x0_im + a0i * x0_re, 0.0)
        for k in range(n_sq):
            ar, ai = pws_ref[0, k:k + 1, c_re], pws_ref[0, k:k + 1, c_im]
            zs_re, zs_im = shift_rows(z_re, 1 << k), shift_rows(z_im, 1 << k)
            z_re, z_im = z_re + (ar * zs_re - ai * zs_im), z_im + (ar * zs_im + ai * zs_re)
        fin_ref[0, 0, :, c_re] = z_re[nc - 1:nc, :]
        fin_ref[0, 0, :, c_im] = z_im[nc - 1:nc, :]
        s_re.append(jnp.where(first, x0_re, pltpu.roll(z_re, 1, 0)).astype(BF16))
        s_im.append(jnp.where(first, x0_im, pltpu.roll(z_im, 1, 0)).astype(BF16))
    s_in = jnp.concatenate(s_re + s_im, axis=1)
    ystk = y_intra + jnp.dot(s_in, wy_ref[0], preferred_element_type=F32)
    for tau in range(t):
        yt_ref[pl.ds(tau, nc, stride=t), :] = ystk[:, tau * lanes:(tau + 1) * lanes]
    o_ref[...] = _gelu_tanh(yt_ref[...] + d_ref[...] * u_ref[...]).astype(o_ref.dtype)


def _s5_prompt(u, tz, ws, wy, pws, x0, d, nb, p):
    m, dm = u.shape
    l = m // nb
    nt, _, sw = ws.shape
    t = SSM_CHUNK
    n_sq = pws.shape[1]
    assert dm == nt * LANES and l % (8 * t) == 0 and (1 << n_sq) >= l // t
    body = functools.partial(_s5_tile_body, t, p, n_sq)
    return pl.pallas_call(
        body,
        grid=(nt, nb),
        in_specs=[
            pl.BlockSpec((l, LANES), lambda n, b: (b, n)),
            pl.BlockSpec((1, t * LANES, t * LANES), lambda n, b: (n, 0, 0)),
            pl.BlockSpec((1, t * LANES, sw), lambda n, b: (n, 0, 0)),
            pl.BlockSpec((1, sw, t * LANES), lambda n, b: (n, 0, 0)),
            pl.BlockSpec((1, n_sq, sw), lambda n, b: (n, 0, 0)),
            pl.BlockSpec((1, 1, 1, sw), lambda n, b: (b, n, 0, 0)),
            pl.BlockSpec((1, LANES), lambda n, b: (0, n)),
        ],
        out_specs=[
            pl.BlockSpec((l, LANES), lambda n, b: (b, n)),
            pl.BlockSpec((1, 1, 1, sw), lambda n, b: (b, n, 0, 0)),
        ],
        out_shape=[jax.ShapeDtypeStruct((m, dm), BF16), jax.ShapeDtypeStruct((nb, nt, 1, sw), F32)],
        scratch_shapes=[pltpu.VMEM((l, LANES), F32)],
        compiler_params=_params(("parallel", "parallel")),
        name="s5_prompt",
    )(u, tz, ws, wy, pws, x0, d.reshape(1, dm))


def _s5_step_body(u_ref, x0r_ref, x0i_ref, abr_ref, abi_ref, wbr_ref, wbi_ref, wcr_ref, wci_ref, d_ref,
                  g_ref, sr_ref, si_ref):
    u = u_ref[...]
    ub = u.astype(BF16)
    bu_re = jnp.dot(ub, wbr_ref[0], preferred_element_type=F32)
    bu_im = jnp.dot(ub, wbi_ref[0], preferred_element_type=F32)
    ar, ai = abr_ref[...], abi_ref[...]
    s0r, s0i = _cmul(ar, ai, x0r_ref[...], x0i_ref[...])
    s_re = bu_re + s0r
    s_im = bu_im + s0i
    sr_ref[...] = s_re
    si_ref[...] = s_im
    y = (jnp.dot(s_re.astype(BF16), wcr_ref[0], preferred_element_type=F32)
         - jnp.dot(s_im.astype(BF16), wci_ref[0], preferred_element_type=F32)
         + d_ref[...] * u)
    g_ref[...] = _gelu_tanh(y)


def _s5_step(u, x0_re, x0_im, ab, bb, c_re, c_im, d):
    bd, dm = u.shape
    g, p = ab.shape[1:]
    c = dm // g
    gl = LANES // c
    nt = g // gl
    sl = gl * p
    eye = jnp.eye(gl, dtype=F32)

    def bdiag_in(w):
        w = w.reshape(nt, gl, c, p)
        return jnp.einsum("nacp,ab->nacbp", w, eye).reshape(nt, gl * c, gl * p).astype(BF16)

    def bdiag_out(w):
        w = w.reshape(nt, gl, c, p)
        return jnp.einsum("nacp,ab->napbc", w, eye).reshape(nt, gl * p, gl * c).astype(BF16)

    row = lambda a: a.reshape(1, g * p)
    lane_blk = pl.BlockSpec((bd, LANES), lambda n: (0, n))
    st_blk = pl.BlockSpec((bd, sl), lambda n: (0, n))
    st_row = pl.BlockSpec((1, sl), lambda n: (0, n))
    w_in_blk = pl.BlockSpec((1, LANES, sl), lambda n: (n, 0, 0))
    w_out_blk = pl.BlockSpec((1, sl, LANES), lambda n: (n, 0, 0))
    return pl.pallas_call(
        _s5_step_body,
        grid=(nt,),
        in_specs=[lane_blk, st_blk, st_blk, st_row, st_row, w_in_blk, w_in_blk, w_out_blk, w_out_blk,
                  pl.BlockSpec((1, LANES), lambda n: (0, n))],
        out_specs=[lane_blk, st_blk, st_blk],
        out_shape=[jax.ShapeDtypeStruct((bd, dm), F32), jax.ShapeDtypeStruct((bd, g * p), F32),
                   jax.ShapeDtypeStruct((bd, g * p), F32)],
        compiler_params=_params(("parallel",)),
        name="s5_step",
    )(u, x0_re.reshape(bd, g * p), x0_im.reshape(bd, g * p), row(ab[0]), row(ab[1]),
      bdiag_in(bb[0]), bdiag_in(bb[1]), bdiag_out(c_re), bdiag_out(c_im), d.reshape(1, dm))


def _cumsum_body(nblk, tb, scale, x_ref, o_ref):
    r = lax.broadcasted_iota(jnp.int32, (tb, tb), 0)
    c = lax.broadcasted_iota(jnp.int32, (tb, tb), 1)
    tri = jnp.where(c <= r, 1.0, 0.0).astype(F32)
    carry = jnp.zeros((1, x_ref.shape[-1]), F32)
    for i in range(nblk):
        blk = jnp.dot(tri, x_ref[0, i * tb:(i + 1) * tb, :], precision=HIGHEST, preferred_element_type=F32) + carry
        o_ref[0, i * tb:(i + 1) * tb, :] = blk * scale
        carry = blk[tb - 1:tb, :]


def _cumsum_seq(x, scale, tb=256):
    b, l, h = x.shape
    tb = min(tb, l)
    blk = pl.BlockSpec((1, l, h), lambda i: (i, 0, 0))
    return pl.pallas_call(
        functools.partial(_cumsum_body, l // tb, tb, scale),
        grid=(b,),
        in_specs=[blk],
        out_specs=blk,
        out_shape=jax.ShapeDtypeStruct((b, l, h), F32),
        compiler_params=_params(("parallel",)),
        name="logf_cumsum",
    )(x)


def _fox_prompt_body(blk, q_ref, k_ref, v_ref, c_ref, ct_ref, o_ref, ck_ref, m_ref, l_ref, acc_ref, s_ref, vt_ref):
    h = pl.program_id(1)
    sub = blk // FOX_ROW_SPLIT
    n_tiles = ct_ref.shape[2]

    c_all = c_ref[0]
    head = lax.broadcasted_iota(jnp.int32, c_all.shape, 1)
    c_head = jnp.sum(jnp.where(head == h, c_all, 0.0), axis=-1, keepdims=True)
    ck_ref[...] = jnp.broadcast_to(c_head, ck_ref.shape)
    m_ref[...] = jnp.full(m_ref.shape, -jnp.inf, F32)
    l_ref[...] = jnp.zeros(l_ref.shape, F32)
    acc_ref[...] = jnp.zeros(acc_ref.shape, F32)
    for j in range(n_tiles):
        vt_ref[j] = v_ref[0, j * blk:(j + 1) * blk, :].astype(F32).T.astype(BF16)

    def logits(n, i, j):
        s_ref[n % 2] = lax.dot_general(k_ref[0, j * blk:(j + 1) * blk, :], q_ref[0, i * blk:(i + 1) * blk, :],
                                       NT_DIMS, preferred_element_type=F32)

    def update(n, i, j):
        causal = i == j
        subs = range(FOX_ROW_SPLIT)
        cols = [slice(r * sub, (r + 1) * sub) for r in subs]
        n_keys = [(r + 1) * sub if causal else blk for r in subs]
        pr, alpha = [], []
        for r in subs:
            c_q = ct_ref[0, 0, i:i + 1, cols[r]]
            c_k = ck_ref[j * blk:j * blk + n_keys[r], :]
            s_r = s_ref[n % 2, :n_keys[r], cols[r]] - jnp.concatenate([c_k] * (sub // LANES), axis=1)
            if causal:
                k_pos = lax.broadcasted_iota(jnp.int32, (n_keys[r], sub), 0)
                q_pos = r * sub + lax.broadcasted_iota(jnp.int32, (n_keys[r], sub), 1)
                s_r = jnp.where(k_pos <= q_pos, s_r, -jnp.inf)
            m_old = m_ref[i, :, cols[r]]
            m_new = jnp.maximum(m_old, jnp.max(s_r, axis=0, keepdims=True) + c_q)
            alpha.append(jnp.exp2(m_old - m_new))
            p_r = jnp.exp2(s_r - (m_new - c_q))
            l_ref[i, :, cols[r]] = alpha[r] * l_ref[i, :, cols[r]] + jnp.sum(p_r, axis=0, keepdims=True)
            m_ref[i, :, cols[r]] = m_new
            pr.append(p_r.astype(BF16))
        for r in subs:
            vt = vt_ref[j, :, :n_keys[r]]
            acc_ref[i, :, cols[r]] = (alpha[r] * acc_ref[i, :, cols[r]]
                                      + jnp.dot(vt, pr[r], preferred_element_type=F32))
        if causal:
            o_ref[0, i * blk:(i + 1) * blk, :] = (acc_ref[i] / l_ref[i]).T.astype(o_ref.dtype)

    pairs = [(i, j) for i in range(n_tiles) for j in range(i + 1)]
    logits(0, *pairs[0])
    for n, (i, j) in enumerate(pairs):
        if n + 1 < len(pairs):
            logits(n + 1, *pairs[n + 1])
        update(n, i, j)


def _fox_prompt(q, k, v, c, n_heads, blk=512):
    b, l, d = q.shape
    dh = d // n_heads
    blk = min(blk, l)
    nblk = l // blk
    ct = jnp.swapaxes(c, 1, 2).reshape(b, n_heads, nblk, blk)
    return pl.pallas_call(
        functools.partial(_fox_prompt_body, blk),
        grid=(b, n_heads),
        in_specs=[
            pl.BlockSpec((1, l, dh), lambda bi, h: (bi, 0, h)),
            pl.BlockSpec((1, l, dh), lambda bi, h: (bi, 0, h)),
            pl.BlockSpec((1, l, dh), lambda bi, h: (bi, 0, h)),
            pl.BlockSpec((1, l, n_heads), lambda bi, h: (bi, 0, 0)),
            pl.BlockSpec((1, 1, nblk, blk), lambda bi, h: (bi, h, 0, 0)),
        ],
        out_specs=pl.BlockSpec((1, l, dh), lambda bi, h: (bi, 0, h)),
        out_shape=jax.ShapeDtypeStruct((b, l, d), BF16),
        scratch_shapes=[pltpu.VMEM((l, LANES), F32), pltpu.VMEM((nblk, 1, blk), F32), pltpu.VMEM((nblk, 1, blk), F32),
                        pltpu.VMEM((nblk, dh, blk), F32), pltpu.VMEM((2, blk, blk), F32),
                        pltpu.VMEM((nblk, dh, blk), BF16)],
        compiler_params=_params(("parallel", "parallel")),
        name="fox_prompt",
    )(q, k, v, c, ct)


def _fox_decode_body(n_heads, dh, n_pp, pt_ref, q_ref, kn_ref, vn_ref, cn_ref, *refs):
    k_refs = refs[:n_pp]
    v_refs = refs[n_pp:2 * n_pp]
    lf_refs = refs[2 * n_pp:3 * n_pp]
    o_ref, m_ref, l_ref, acc_ref, carry_ref = refs[3 * n_pp:]
    step = pl.program_id(1)
    scale = dh ** -0.5
    q = q_ref[0].astype(BF16)

    @pl.when(step == 0)
    def _():
        kn = kn_ref[0].astype(BF16).astype(F32)
        m_ref[...] = jnp.sum(q.astype(F32) * kn, axis=-1, keepdims=True) * scale
        l_ref[...] = jnp.ones(l_ref.shape, F32)
        acc_ref[...] = vn_ref[0]
        carry_ref[...] = jnp.zeros(carry_ref.shape, F32)

    n_r, lanes = lf_refs[0].shape[1:]
    lane = lax.broadcasted_iota(jnp.int32, (n_r, lanes), 1)
    r_i = lax.broadcasted_iota(jnp.int32, (n_r, n_r), 0)
    c_i = lax.broadcasted_iota(jnp.int32, (n_r, n_r), 1)
    later_row = jnp.where(c_i > r_i, 1.0, 0.0).astype(F32)
    row_head = lax.broadcasted_iota(jnp.int32, (n_heads, lanes), 0)
    col_head = lax.broadcasted_iota(jnp.int32, (n_heads, lanes), 1) % n_heads
    own = row_head == col_head
    cn = cn_ref[0]

    for k_ref, v_ref, lf_ref in zip(k_refs, v_refs, lf_refs):
        lf = lf_ref[0]
        same_head_total = lf
        later_in_row = jnp.zeros_like(lf)
        for k in range(1, lanes // n_heads):
            same_head_total = same_head_total + pltpu.roll(lf, k * n_heads, 1)
            later_in_row = later_in_row + jnp.where(lane + k * n_heads < lanes,
                                                    pltpu.roll(lf, lanes - k * n_heads, 1), 0.0)
        later_rows = jnp.dot(later_row, same_head_total, precision=HIGHEST, preferred_element_type=F32)
        decay = later_in_row + later_rows + carry_ref[...]
        carry_ref[...] += jnp.sum(same_head_total, axis=0, keepdims=True)

        s = lax.dot_general(q, k_ref[0].astype(BF16), NT_DIMS, preferred_element_type=F32)
        s = jnp.concatenate(
            [jnp.where(own, s[:, r * lanes:(r + 1) * lanes] * scale + cn + decay[r:r + 1, :], -jnp.inf)
             for r in range(n_r)], axis=1)
        m_old = m_ref[...]
        m_new = jnp.maximum(m_old, jnp.max(s, axis=-1, keepdims=True))
        alpha = jnp.exp(m_old - m_new)
        pr = jnp.exp(s - m_new)
        l_ref[...] = alpha * l_ref[...] + jnp.sum(pr, axis=-1, keepdims=True)
        acc_ref[...] = alpha * acc_ref[...] + jnp.dot(pr.astype(BF16), v_ref[0].astype(BF16),
                                                      preferred_element_type=F32)
        m_ref[...] = m_new

    @pl.when(step == pl.num_programs(1) - 1)
    def _():
        o_ref[0] = acc_ref[...] / l_ref[...]


def _fox_decode(q, k_new, v_new, logf_new, cache_k, cache_v, cache_logf, page_table, n_pp=4):
    bd, n_heads, dh = q.shape
    n_pool, ps = cache_k.shape[:2]
    n_pages = page_table.shape[1]
    rows = ps * n_heads
    n_pp = math.gcd(n_pp, n_pages)
    assert LANES % n_heads == 0 and rows % LANES == 0
    tok = pl.BlockSpec((1, n_heads, dh), lambda b, s, pt: (b, 0, 0))

    def page_specs(block):
        return [pl.BlockSpec(block, functools.partial(
            lambda b, s, pt, i: (pt[b, n_pages - 1 - (s * n_pp + i)], 0, 0), i=i)) for i in range(n_pp)]

    grid_spec = pltpu.PrefetchScalarGridSpec(
        num_scalar_prefetch=1,
        grid=(bd, n_pages // n_pp),
        in_specs=[tok, tok, tok, pl.BlockSpec((1, n_heads, 1), lambda b, s, pt: (b, 0, 0)),
                  *page_specs((1, rows, dh)), *page_specs((1, rows, dh)),
                  *page_specs((1, rows // LANES, LANES))],
        out_specs=tok,
        scratch_shapes=[pltpu.VMEM((n_heads, 1), F32), pltpu.VMEM((n_heads, 1), F32),
                        pltpu.VMEM((n_heads, dh), F32), pltpu.VMEM((1, LANES), F32)],
    )
    return pl.pallas_call(
        functools.partial(_fox_decode_body, n_heads, dh, n_pp),
        grid_spec=grid_spec,
        out_shape=jax.ShapeDtypeStruct((bd, n_heads, dh), F32),
        compiler_params=_params(("parallel", "arbitrary")),
        name="fox_decode",
    )(page_table, q, k_new, v_new, logf_new.reshape(bd, n_heads, 1),
      *([cache_k.reshape(n_pool, rows, dh)] * n_pp), *([cache_v.reshape(n_pool, rows, dh)] * n_pp),
      *([cache_logf.reshape(n_pool, rows // LANES, LANES)] * n_pp))


class _Weights:
    def __init__(self, bn_first, **tensors):
        self.bn_first = bn_first
        self.t = tensors
        self.copies = {}

    def matmul(self, x, names, layer, offsets, n_cols, epi, out_dtypes, **kw):
        key = (tuple(names), layer)
        if key in self.copies:
            return _matmul(x, [(wb, 0) for wb in self.copies[key]], n_cols, epi, out_dtypes, **kw)
        stacked = [self.t[n] if self.t[n].ndim == 3 else self.t[n][None] for n in names]
        outs, self.copies[key] = _matmul_cast_first(x, list(zip(stacked, offsets)), layer, n_cols, epi, out_dtypes,
                                                    self.bn_first, **kw)
        return outs


def _ffn(h, w, layer, t):
    d_ff = w.t["ffn_w_down"].shape[1]
    (xn,) = _rmsnorm(h, [w.t["norm_ffn"][layer]], t["act"], t["bm_norm"])
    (a,) = w.matmul(xn, ["ffn_w_gate_up"] * 2, layer, (0, d_ff), d_ff, _epi_swiglu, (t["act"],),
                    bm=t["bm"], bn=t["bn_ff"])
    (h,) = w.matmul(a, ["ffn_w_down"], layer, (0,), h.shape[1], _epi_residual, (F32,), bm=t["bm"], bn=t["bn_down"],
                    bk=t["bk_down"], tiles=(h,))
    return h


def _trunk_front(x, w, s5, t, q_scale):
    m, dm = x.shape
    (xn,) = _rmsnorm(x, [w.t["norm_mix"][0]], t["act"], t["bm_norm"])
    (u,) = w.matmul(xn, ["ssm_w_in"], 0, (0,), dm, _epi_plain, (F32,), bm=t["bm"], bn=t["bn"])
    g_act, fin = s5(u)
    (h,) = w.matmul(g_act, ["ssm_w_glu"] * 2, 0, (0, dm), dm, _epi_glu_residual, (F32,), bm=t["bm"], bn=t["bn_glu"],
                    tiles=(x,))
    h = _ffn(h, w, 0, t)
    z, qn = _rmsnorm(h, [w.t["norm_kv"], w.t["norm_mix"][1]], t["act"], t["bm_norm"])
    k, kb, v, vb = w.matmul(z, ["kv_w_k", "kv_w_v"], 0, (0, 0), dm, _epi_kv, (F32, t["act"], F32, t["act"]),
                            bm=t["bm"], bn=t["bn_kv"])
    n_heads = w.t["kv_w_f"].shape[1]
    (logf,) = _matmul(z, [(w.t["kv_w_f"].astype(BF16), 0)], n_heads, _epi_log_sigmoid, (F32,), bm=t["bm"], bn=n_heads,
                      rows=(w.t["kv_b_f"].reshape(1, n_heads),))
    q_epi = _epi_plain if q_scale is None else functools.partial(_epi_scaled, q_scale)
    (q,) = w.matmul(qn, ["attn_w_q"], 0, (0,), dm, q_epi, (t["act"],), bm=t["bm"], bn=t["bn"])
    return h, q, k, v, kb, vb, logf, fin


def _trunk_back(h, att, w, t):
    (h,) = w.matmul(att, ["attn_w_o"], 0, (0,), h.shape[1], _epi_residual, (F32,), bm=t["bm"], bn=t["bn"], tiles=(h,))
    h = _ffn(h, w, 1, t)
    (y,) = _rmsnorm(h, [w.t["norm_final"]], F32, t["bm_norm"])
    return y


def _down_k_block(d_ff):
    units = d_ff // LANES
    for parts in range(2, units + 1):
        if units % parts == 0:
            return d_ff // parts
    return d_ff


def kernel(x_prompt, x_sample, state_ssm_re, state_ssm_im, cache_k, cache_v, cache_logf, page_table, norm_mix, norm_ffn, ssm_w_in, ssm_lambda_re, ssm_lambda_im, ssm_log_dt, ssm_b_re, ssm_b_im, ssm_c_re, ssm_c_im, ssm_d, ssm_w_glu, attn_w_q, attn_w_o, norm_kv, kv_w_k, kv_w_v, kv_w_f, kv_b_f, ffn_w_gate_up, ffn_w_down, norm_final):
    b, l, dm = x_prompt.shape
    bd, t_dec, _ = x_sample.shape
    assert t_dec == 1, "the sample group decodes one token per sequence"
    assert ssm_w_in.shape[0] == 1 and attn_w_q.shape[0] == 1, "one S5 layer followed by one FoX layer"
    g, p = ssm_lambda_re.shape[1:]
    n_heads = kv_w_f.shape[1]
    dh = dm // n_heads
    d_ff = ffn_w_down.shape[1]
    nc = l // SSM_CHUNK
    n_sq = max(1, (nc - 1).bit_length())

    w = _Weights(
        256,
        norm_mix=norm_mix, norm_ffn=norm_ffn, norm_kv=norm_kv, norm_final=norm_final, kv_b_f=kv_b_f, kv_w_f=kv_w_f,
        ssm_w_in=ssm_w_in, ssm_w_glu=ssm_w_glu, attn_w_q=attn_w_q, attn_w_o=attn_w_o, kv_w_k=kv_w_k, kv_w_v=kv_w_v,
        ffn_w_gate_up=ffn_w_gate_up, ffn_w_down=ffn_w_down,
    )
    bn_ff = math.gcd(d_ff, 1024)
    bk_down = _down_k_block(d_ff)
    tile_p = dict(act=BF16, bm=1024, bn=512, bn_kv=256, bn_glu=256, bn_ff=bn_ff, bn_down=512, bk_down=bk_down,
                  bm_norm=256)
    tile_s = dict(act=F32, bm=bd, bn=1024, bn_kv=512, bn_glu=512, bn_ff=bn_ff, bn_down=1024, bk_down=bk_down,
                  bm_norm=bd)

    tz, ws, wy, pws, ab, bb = _s5_prep(
        ssm_lambda_re[0], ssm_lambda_im[0], ssm_log_dt[0], ssm_b_re[0], ssm_b_im[0], ssm_c_re[0], ssm_c_im[0], n_sq)
    nt, _, sw = ws.shape

    def s5_prompt(u):
        x0 = jnp.zeros((b, nt, 1, sw), F32)
        return _s5_prompt(u, tz, ws, wy, pws, x0, ssm_d[0], b, p)

    xp = x_prompt.reshape(b * l, dm)
    h, q, k, v, kb, vb, logf, fin = _trunk_front(xp, w, s5_prompt, tile_p, dh ** -0.5 * LOG2E)
    cum = _cumsum_seq(logf.reshape(b, l, n_heads), LOG2E)
    att = _fox_prompt(q.reshape(b, l, dm), kb.reshape(b, l, dm), vb.reshape(b, l, dm), cum, n_heads)
    y_prompt = _trunk_back(h, att.reshape(b * l, dm), w, tile_p).reshape(b, l, dm)
    fin = fin.reshape(b, nt, 2, sw // 2)
    ssm_re_prompt = fin[:, :, 0].reshape(1, b, g, p)
    ssm_im_prompt = fin[:, :, 1].reshape(1, b, g, p)
    k_prompt = k.reshape(b, l, n_heads, dh)
    v_prompt = v.reshape(b, l, n_heads, dh)
    logf_prompt = logf.reshape(b, l, n_heads)

    def s5_sample(u):
        g_act, s_re, s_im = _s5_step(u, state_ssm_re[0], state_ssm_im[0], ab, bb, ssm_c_re[0], ssm_c_im[0], ssm_d[0])
        return g_act, (s_re, s_im)

    xs = x_sample.reshape(bd, dm)
    hs, qs, ks, vs, _, _, logfs, (s_re, s_im) = _trunk_front(xs, w, s5_sample, tile_s, None)
    att_s = _fox_decode(qs.reshape(bd, n_heads, dh), ks.reshape(bd, n_heads, dh), vs.reshape(bd, n_heads, dh),
                        logfs, cache_k, cache_v, cache_logf, page_table)
    y_sample = _trunk_back(hs, att_s.reshape(bd, dm), w, tile_s).reshape(bd, 1, dm)
    ssm_re_sample = s_re.reshape(1, bd, g, p)
    ssm_im_sample = s_im.reshape(1, bd, g, p)
    k_sample = ks.reshape(bd, 1, n_heads, dh)
    v_sample = vs.reshape(bd, 1, n_heads, dh)
    logf_sample = logfs.reshape(bd, 1, n_heads)

    return (y_prompt, y_sample, ssm_re_prompt, ssm_im_prompt, k_prompt, v_prompt, logf_prompt,
            ssm_re_sample, ssm_im_sample, k_sample, v_sample, logf_sample)
```

```python
import functools
import math

import jax
import jax.numpy as jnp
from jax import lax
from jax.experimental import pallas as pl
from jax.experimental.pallas import tpu as pltpu

F32 = jnp.float32
BF16 = jnp.bfloat16
RMS_EPS = 1e-6
LANES = 128
SSM_CHUNK = 8
FOX_ROW_SPLIT = 2
V7X_VMEM_LIMIT = 56 * 1024 * 1024
HIGHEST = lax.Precision.HIGHEST
LOG2E = math.log2(math.e)
NT_DIMS = (((1,), (1,)), ((), ()))


def _params(semantics, vmem_bytes=V7X_VMEM_LIMIT):
    return pltpu.CompilerParams(dimension_semantics=semantics, vmem_limit_bytes=vmem_bytes)


def _sigmoid(x):
    return 1.0 / (1.0 + jnp.exp(-x))


def _rmsnorm_body(n, x_ref, *refs):
    x = x_ref[...]
    y = x * lax.rsqrt(jnp.mean(x * x, axis=-1, keepdims=True) + RMS_EPS)
    for g_ref, o_ref in zip(refs[:n], refs[n:]):
        o_ref[...] = (y * g_ref[...]).astype(o_ref.dtype)


def _rmsnorm(x, gains, out_dtype, bm):
    m, d = x.shape
    bm = min(bm, m)
    n = len(gains)
    blk = pl.BlockSpec((bm, d), lambda i: (i, 0))
    return pl.pallas_call(
        functools.partial(_rmsnorm_body, n),
        grid=(m // bm,),
        in_specs=[blk] + [pl.BlockSpec((1, d), lambda i: (0, 0))] * n,
        out_specs=[blk] * n,
        out_shape=[jax.ShapeDtypeStruct((m, d), out_dtype)] * n,
        compiler_params=_params(("parallel",)),
        name="rmsnorm",
    )(x, *[g.reshape(1, d) for g in gains])


def _mm_body(nw, n_tile, n_row, n_prev, n_out, n_wb, nk, epi, *refs):
    pos = iter(range(len(refs)))
    take = lambda n: [refs[next(pos)] for _ in range(n)]
    (x_ref,), w_refs, t_refs, r_refs = take(1), take(nw), take(n_tile), take(n_row)
    take(n_prev)
    o_refs, wb_refs, acc_refs = take(n_out), take(n_wb), take(nw if nk > 1 else 0)
    x = x_ref[...].astype(BF16)
    ws = [w[...].astype(BF16) for w in w_refs]
    for wb_ref, wv in zip(wb_refs, ws):
        wb_ref[...] = wv
    parts = [jnp.dot(x, wv, preferred_element_type=F32) for wv in ws]

    def finish(accs):
        outs = epi(accs, [t[...] for t in t_refs], [r[...] for r in r_refs])
        for o, v in zip(o_refs, outs):
            o[...] = v.astype(o.dtype)

    if nk == 1:
        finish(parts)
    else:
        k = pl.program_id(2)

        @pl.when(k == 0)
        def _():
            for a, p in zip(acc_refs, parts):
                a[...] = p

        @pl.when(jnp.logical_and(k > 0, k < nk - 1))
        def _():
            for a, p in zip(acc_refs, parts):
                a[...] += p

        @pl.when(k == nk - 1)
        def _():
            finish([a[...] + p for a, p in zip(acc_refs, parts)])


def _matmul(x, weights, n_cols, epi, out_dtypes, *, bm, bn, bk=None, tiles=(), rows=(),
            row_blocks=None, prev_outs=(), cast_layer=None):
    m, kdim = x.shape
    bm = min(bm, m)
    bn = min(bn, n_cols)
    bk = kdim if bk is None else bk
    assert m % bm == 0 and n_cols % bn == 0 and kdim % bk == 0, (x.shape, bm, bn, bk)
    assert all(off % bn == 0 for _, off in weights)
    i0, n_i = (0, m // bm) if row_blocks is None else row_blocks
    nj, nk = n_cols // bn, kdim // bk
    nw = len(weights)
    casting = cast_layer is not None
    in_specs = [pl.BlockSpec((bm, bk), lambda i, j, k: (i + i0, k))]
    for _, off in weights:
        if casting:
            in_specs.append(pl.BlockSpec(
                (None, bk, bn), functools.partial(lambda i, j, k, o: (cast_layer, k, j + o), o=off // bn)))
        else:
            in_specs.append(pl.BlockSpec((bk, bn), functools.partial(lambda i, j, k, o: (k, j + o), o=off // bn)))
    out_blk = pl.BlockSpec((bm, bn), lambda i, j, k: (i + i0, j))
    in_specs += [out_blk for _ in tiles]
    in_specs += [pl.BlockSpec((1, bn), lambda i, j, k: (0, j)) for _ in rows]
    in_specs += [pl.BlockSpec(memory_space=pl.ANY) for _ in prev_outs]
    n_in = 1 + nw + len(tiles) + len(rows)
    out_specs = [out_blk for _ in out_dtypes]
    out_shape = [jax.ShapeDtypeStruct((m, n_cols), dt) for dt in out_dtypes]
    if casting:
        assert n_i == 1, "each weight block must be written exactly once"
        out_specs += [pl.BlockSpec((bk, bn), lambda i, j, k: (k, j)) for _ in weights]
        out_shape += [jax.ShapeDtypeStruct((kdim, n_cols), BF16) for _ in weights]
    body = functools.partial(_mm_body, nw, len(tiles), len(rows), len(prev_outs), len(out_dtypes),
                             nw if casting else 0, nk, epi)
    outs = pl.pallas_call(
        body,
        grid=(n_i, nj, nk),
        in_specs=in_specs,
        out_specs=out_specs,
        out_shape=out_shape,
        scratch_shapes=[pltpu.VMEM((bm, bn), F32) for _ in range(nw if nk > 1 else 0)],
        input_output_aliases={n_in + a: a for a in range(len(prev_outs))},
        compiler_params=_params(("parallel", "parallel", "arbitrary")),
        name="matmul_" + getattr(epi, "__name__", "epi").strip("_") + ("_cast" if casting else ""),
    )(x, *[w for w, _ in weights], *tiles, *rows, *prev_outs)
    if casting:
        return outs[:len(out_dtypes)], outs[len(out_dtypes):]
    return outs


def _matmul_cast_first(x, weights, layer, n_cols, epi, out_dtypes, bn_first, **kw):
    m = x.shape[0]
    bm = min(kw["bm"], m)
    outs, copies = _matmul(x, weights, n_cols, epi, out_dtypes, row_blocks=(0, 1), cast_layer=layer,
                           **{**kw, "bn": min(bn_first, kw["bn"])})
    if m > bm:
        outs = _matmul(x, [(wb, 0) for wb in copies], n_cols, epi, out_dtypes, row_blocks=(1, m // bm - 1),
                       prev_outs=outs, **kw)
    return outs, copies


def _epi_plain(accs, tiles, rows):
    return (accs[0],)


def _epi_kv(accs, tiles, rows):
    k, v = accs
    return (k, k, v, v)


def _epi_scaled(scale, accs, tiles, rows):
    return (accs[0] * scale,)


def _epi_residual(accs, tiles, rows):
    return (tiles[0] + accs[0],)


def _epi_swiglu(accs, tiles, rows):
    g, u = accs
    return (g * _sigmoid(g) * u,)


def _epi_glu_residual(accs, tiles, rows):
    za, zb = accs
    return (tiles[0] + za * _sigmoid(zb),)


def _epi_log_sigmoid(accs, tiles, rows):
    z = accs[0] + rows[0]
    return (jnp.minimum(z, 0.0) - jnp.log1p(jnp.exp(-jnp.abs(z))),)


def _cmul(ar, ai, br, bi):
    return ar * br - ai * bi, ar * bi + ai * br


def _s5_prep_body(t, c, n_sq, lre_ref, lim_ref, ldt_ref, btr_ref, bti_ref, cre_ref, cim_ref,
                  tz_ref, ws_ref, wy_ref, pws_ref, ab_ref, wc_ref):
    rows, p = lre_ref.shape
    sw = ws_ref.shape[-1]
    lam_re = lre_ref[...]
    lam_im = lim_ref[...]
    dt = jnp.exp(ldt_ref[...])
    mag = jnp.exp(lam_re * dt)
    ab_re = mag * jnp.cos(lam_im * dt)
    ab_im = mag * jnp.sin(lam_im * dt)
    den = lam_re * lam_re + lam_im * lam_im
    co_re = ((ab_re - 1.0) * lam_re + ab_im * lam_im) / den
    co_im = (ab_im * lam_re - (ab_re - 1.0) * lam_im) / den
    bb_re, bb_im = _cmul(co_re, co_im, btr_ref[...], bti_ref[...])
    c_re = cre_ref[...]
    c_im = cim_ref[...]
    pw = [(jnp.ones_like(ab_re), jnp.zeros_like(ab_im))]
    for _ in range(t):
        pw.append(_cmul(pw[-1][0], pw[-1][1], ab_re, ab_im))

    def iota(shape, dim):
        return lax.broadcasted_iota(jnp.int32, shape, dim)

    same_group = iota((rows, rows), 0) // c == iota((rows, rows), 1) // c
    gl = sw // (2 * p)
    row_in_col_group = iota((rows, sw), 0) // c == (iota((rows, sw), 1) % (gl * p)) // p
    first_row_of_group = jnp.logical_and(row_in_col_group, iota((rows, sw), 0) % c == 0)

    def dot_nt_hi(a, b):
        return lax.dot_general(a, b, NT_DIMS, precision=HIGHEST, preferred_element_type=F32)

    def spread(w_re, w_im):
        return jnp.where(row_in_col_group, jnp.concatenate([w_re] * gl + [w_im] * gl, axis=1), 0.0)

    tz_ref[0] = jnp.zeros(tz_ref.shape[1:], tz_ref.dtype)
    for d in range(t):
        pr, pi = pw[d]
        m_re, m_im = _cmul(c_re, c_im, pr, pi)
        kd_t = jnp.where(same_group, dot_nt_hi(bb_re, m_re) - dot_nt_hi(bb_im, m_im), 0.0).astype(tz_ref.dtype)
        for tau in range(t - d):
            tz_ref[0, tau * rows:(tau + 1) * rows, (tau + d) * rows:(tau + d + 1) * rows] = kd_t
        w_re, w_im = _cmul(bb_re, bb_im, pr, pi)
        tau = t - 1 - d
        ws_ref[0, tau * rows:(tau + 1) * rows, :] = spread(w_re, w_im).astype(ws_ref.dtype)
        qr, qi = pw[d + 1]
        y_re, y_im = _cmul(c_re, c_im, qr, qi)
        wy_ref[0, :, d * rows:(d + 1) * rows] = spread(y_re, -y_im).T.astype(wy_ref.dtype)

    def state_row(w_re, w_im):
        return jnp.sum(jnp.where(first_row_of_group, spread(w_re, w_im), 0.0), axis=0, keepdims=True)

    qr, qi = pw[t]
    for k in range(n_sq):
        pws_ref[0, k:k + 1, :] = state_row(qr, qi)
        qr, qi = _cmul(qr, qi, qr, qi)
    ab_ref[0] = state_row(ab_re, ab_im)
    wc_ref[0] = spread(c_re, -c_im).T.astype(wc_ref.dtype)


def _s5_prep(lam_re, lam_im, log_dt, b_re, b_im, c_re, c_im, n_sq):
    g, p = lam_re.shape
    c = c_re.shape[1]
    t = SSM_CHUNK
    rows = LANES
    gl = rows // c
    nt = g // gl
    sw = gl * 2 * p
    per_row = lambda a: jnp.repeat(a, c, axis=0)
    bt_re = jnp.swapaxes(b_re, 1, 2).reshape(g * c, p)
    bt_im = jnp.swapaxes(b_im, 1, 2).reshape(g * c, p)
    rp = pl.BlockSpec((rows, p), lambda n: (n, 0))
    out_shapes = [
        jax.ShapeDtypeStruct((nt, t * rows, t * rows), BF16),
        jax.ShapeDtypeStruct((nt, t * rows, sw), BF16),
        jax.ShapeDtypeStruct((nt, sw, t * rows), BF16),
        jax.ShapeDtypeStruct((nt, n_sq, sw), F32),
        jax.ShapeDtypeStruct((nt, 1, sw), F32),
        jax.ShapeDtypeStruct((nt, sw, rows), BF16),
    ]
    out_specs = [
        pl.BlockSpec((1, t * rows, t * rows), lambda n: (n, 0, 0)),
        pl.BlockSpec((1, t * rows, sw), lambda n: (n, 0, 0)),
        pl.BlockSpec((1, sw, t * rows), lambda n: (n, 0, 0)),
        pl.BlockSpec((1, n_sq, sw), lambda n: (n, 0, 0)),
        pl.BlockSpec((1, 1, sw), lambda n: (n, 0, 0)),
        pl.BlockSpec((1, sw, rows), lambda n: (n, 0, 0)),
    ]
    return pl.pallas_call(
        functools.partial(_s5_prep_body, t, c, n_sq),
        grid=(nt,),
        in_specs=[rp, rp, pl.BlockSpec((rows, 1), lambda n: (n, 0)), rp, rp, rp, rp],
        out_specs=out_specs,
        out_shape=out_shapes,
        compiler_params=_params(("parallel",)),
        name="s5_prep",
    )(per_row(lam_re), per_row(lam_im), per_row(log_dt.reshape(g, 1)), bt_re, bt_im,
      c_re.reshape(g * c, p), c_im.reshape(g * c, p))


def _gelu_tanh(y):
    return 0.5 * y * (1.0 + jnp.tanh(math.sqrt(2.0 / math.pi) * (y + 0.044715 * (y * y * y))))


def _s5_tile_body(t, p, n_sq, u_ref, tz_ref, ws_ref, wy_ref, pws_ref, x0_ref, d_ref, o_ref, fin_ref, yt_ref):
    l, lanes = u_ref.shape
    nc = l // t
    sw = ws_ref.shape[-1]
    ustk = jnp.concatenate(
        [u_ref[pl.ds(tau, nc, stride=t), :].astype(BF16) for tau in range(t)], axis=1)
    e = jnp.dot(ustk, ws_ref[0], preferred_element_type=F32)
    y_intra = jnp.dot(ustk, tz_ref[0], preferred_element_type=F32)
    half = sw // 2
    row = lax.broadcasted_iota(jnp.int32, (nc, lanes), 0)

    def shift_rows(z, sh):
        if sh % 8 == 0:
            return jnp.concatenate([jnp.zeros((sh, lanes), F32), z[:nc - sh]], axis=0)
        return jnp.where(row >= sh, pltpu.roll(z, sh, 0), 0.0)

    s_re, s_im = [], []
    for ci in range(half // lanes):
        c_re = slice(ci * lanes, (ci + 1) * lanes)
        c_im = slice(half + ci * lanes, half + (ci + 1) * lanes)
        x0_re, x0_im = x0_ref[0, 0, :, c_re], x0_ref[0, 0, :, c_im]
        a0r, a0i = pws_ref[0, 0:1, c_re], pws_ref[0, 0:1, c_im]
        first = row == 0
        z_re = e[:, c_re] + jnp.where(first, a0r * x0_re - a0i * x0_im, 0.0)
        z_im = e[:, c_im] + jnp.where(first, a0r * x0_im + a0i * x0_re, 0.0)
        for k in range(n_sq):
            ar, ai = pws_ref[0, k:k + 1, c_re], pws_ref[0, k:k + 1, c_im]
            zs_re, zs_im = shift_rows(z_re, 1 << k), shift_rows(z_im, 1 << k)
            z_re, z_im = z_re + (ar * zs_re - ai * zs_im), z_im + (ar * zs_im + ai * zs_re)
        fin_ref[0, 0, :, c_re] = z_re[nc - 1:nc, :]
        fin_ref[0, 0, :, c_im] = z_im[nc - 1:nc, :]
        s_re.append(jnp.where(first, x0_re, pltpu.roll(z_re, 1, 0)).astype(BF16))
        s_im.append(jnp.where(first, x0_im, pltpu.roll(z_im, 1, 0)).astype(BF16))
    s_in = jnp.concatenate(s_re + s_im, axis=1)
    ystk = y_intra + jnp.dot(s_in, wy_ref[0], preferred_element_type=F32)
    for tau in range(t):
        yt_ref[pl.ds(tau, nc, stride=t), :] = ystk[:, tau * lanes:(tau + 1) * lanes]
    o_ref[...] = _gelu_tanh(yt_ref[...] + d_ref[...] * u_ref[...]).astype(o_ref.dtype)


def _s5_prompt(u, tz, ws, wy, pws, x0, d, nb, p):
    m, dm = u.shape
    l = m // nb
    nt, _, sw = ws.shape
    t = SSM_CHUNK
    n_sq = pws.shape[1]
    assert dm == nt * LANES and l % (8 * t) == 0 and (1 << n_sq) >= l // t
    body = functools.partial(_s5_tile_body, t, p, n_sq)
    return pl.pallas_call(
        body,
        grid=(nt, nb),
        in_specs=[
            pl.BlockSpec((l, LANES), lambda n, b: (b, n)),
            pl.BlockSpec((1, t * LANES, t * LANES), lambda n, b: (n, 0, 0)),
            pl.BlockSpec((1, t * LANES, sw), lambda n, b: (n, 0, 0)),
            pl.BlockSpec((1, sw, t * LANES), lambda n, b: (n, 0, 0)),
            pl.BlockSpec((1, n_sq, sw), lambda n, b: (n, 0, 0)),
            pl.BlockSpec((1, 1, 1, sw), lambda n, b: (b, n, 0, 0)),
            pl.BlockSpec((1, LANES), lambda n, b: (0, n)),
        ],
        out_specs=[
            pl.BlockSpec((l, LANES), lambda n, b: (b, n)),
            pl.BlockSpec((1, 1, 1, sw), lambda n, b: (b, n, 0, 0)),
        ],
        out_shape=[jax.ShapeDtypeStruct((m, dm), BF16), jax.ShapeDtypeStruct((nb, nt, 1, sw), F32)],
        scratch_shapes=[pltpu.VMEM((l, LANES), F32)],
        compiler_params=_params(("parallel", "parallel")),
        name="s5_prompt",
    )(u, tz, ws, wy, pws, x0, d.reshape(1, dm))


def _s5_step_body(u_ref, x0_ref, ab_ref, wb_ref, wc_ref, d_ref, g_ref, s_ref):
    half = x0_ref.shape[-1] // 2
    u = u_ref[...]
    bu = jnp.dot(u.astype(BF16), wb_ref[0], preferred_element_type=F32)
    x0 = x0_ref[0]
    ab = ab_ref[0]
    s0r, s0i = _cmul(ab[:, :half], ab[:, half:], x0[:, :half], x0[:, half:])
    s = bu + jnp.concatenate([s0r, s0i], axis=1)
    s_ref[0] = s
    y = jnp.dot(s.astype(BF16), wc_ref[0], preferred_element_type=F32) + d_ref[...] * u
    g_ref[...] = _gelu_tanh(y)


def _s5_step(u, x0_re, x0_im, ws, ab, wc, d):
    bd, dm = u.shape
    g, p = x0_re.shape[1:]
    nt, _, sw = ws.shape
    t = SSM_CHUNK
    tile_cols = lambda a: a.reshape(bd, nt, sw // 2)
    x0 = jnp.concatenate([tile_cols(x0_re), tile_cols(x0_im)], axis=-1).transpose(1, 0, 2)
    lane_blk = pl.BlockSpec((bd, LANES), lambda n: (0, n))
    st_blk = pl.BlockSpec((1, bd, sw), lambda n: (n, 0, 0))
    g_act, s = pl.pallas_call(
        _s5_step_body,
        grid=(nt,),
        in_specs=[lane_blk, st_blk, pl.BlockSpec((1, 1, sw), lambda n: (n, 0, 0)),
                  pl.BlockSpec((1, LANES, sw), lambda n: (n, t - 1, 0)),
                  pl.BlockSpec((1, sw, LANES), lambda n: (n, 0, 0)),
                  pl.BlockSpec((1, LANES), lambda n: (0, n))],
        out_specs=[lane_blk, st_blk],
        out_shape=[jax.ShapeDtypeStruct((bd, dm), F32), jax.ShapeDtypeStruct((nt, bd, sw), F32)],
        compiler_params=_params(("parallel",)),
        name="s5_step",
    )(u, x0, ab, ws, wc, d.reshape(1, dm))
    s = s.transpose(1, 0, 2).reshape(bd, nt, 2, sw // 2)
    return g_act, s[:, :, 0].reshape(bd, g, p), s[:, :, 1].reshape(bd, g, p)


def _cumsum_body(nblk, tb, scale, x_ref, o_ref):
    r = lax.broadcasted_iota(jnp.int32, (tb, tb), 0)
    c = lax.broadcasted_iota(jnp.int32, (tb, tb), 1)
    tri = jnp.where(c <= r, 1.0, 0.0).astype(F32)
    carry = jnp.zeros((1, x_ref.shape[-1]), F32)
    for i in range(nblk):
        blk = jnp.dot(tri, x_ref[0, i * tb:(i + 1) * tb, :], precision=HIGHEST, preferred_element_type=F32) + carry
        o_ref[0, i * tb:(i + 1) * tb, :] = blk * scale
        carry = blk[tb - 1:tb, :]


def _cumsum_seq(x, scale, tb=256):
    b, l, h = x.shape
    tb = min(tb, l)
    blk = pl.BlockSpec((1, l, h), lambda i: (i, 0, 0))
    return pl.pallas_call(
        functools.partial(_cumsum_body, l // tb, tb, scale),
        grid=(b,),
        in_specs=[blk],
        out_specs=blk,
        out_shape=jax.ShapeDtypeStruct((b, l, h), F32),
        compiler_params=_params(("parallel",)),
        name="logf_cumsum",
    )(x)


def _fox_prompt_body(blk, q_ref, k_ref, v_ref, c_ref, ct_ref, o_ref, ck_ref, m_ref, l_ref, acc_ref, s_ref, vt_ref):
    h = pl.program_id(1)
    sub = blk // FOX_ROW_SPLIT
    n_tiles = ct_ref.shape[2]

    c_all = c_ref[0]
    head = lax.broadcasted_iota(jnp.int32, c_all.shape, 1)
    c_head = jnp.sum(jnp.where(head == h, c_all, 0.0), axis=-1, keepdims=True)
    ck_ref[...] = jnp.broadcast_to(c_head, ck_ref.shape)
    m_ref[...] = jnp.full(m_ref.shape, -jnp.inf, F32)
    l_ref[...] = jnp.zeros(l_ref.shape, F32)
    acc_ref[...] = jnp.zeros(acc_ref.shape, F32)
    for j in range(n_tiles):
        vt_ref[j] = v_ref[0, j * blk:(j + 1) * blk, :].astype(F32).T.astype(BF16)

    def logits(n, i, j):
        s_ref[n % 2] = lax.dot_general(k_ref[0, j * blk:(j + 1) * blk, :], q_ref[0, i * blk:(i + 1) * blk, :],
                                       NT_DIMS, preferred_element_type=F32)

    def update(n, i, j):
        causal = i == j
        subs = range(FOX_ROW_SPLIT)
        cols = [slice(r * sub, (r + 1) * sub) for r in subs]
        n_keys = [(r + 1) * sub if causal else blk for r in subs]
        pr, alpha = [], []
        for r in subs:
            c_q = ct_ref[0, 0, i:i + 1, cols[r]]
            c_k = ck_ref[j * blk:j * blk + n_keys[r], :]
            s_r = s_ref[n % 2, :n_keys[r], cols[r]] - jnp.concatenate([c_k] * (sub // LANES), axis=1)
            if causal:
                k_pos = lax.broadcasted_iota(jnp.int32, (n_keys[r], sub), 0)
                q_pos = r * sub + lax.broadcasted_iota(jnp.int32, (n_keys[r], sub), 1)
                s_r = jnp.where(k_pos <= q_pos, s_r, -jnp.inf)
            m_old = m_ref[i, :, cols[r]]
            m_new = jnp.maximum(m_old, jnp.max(s_r, axis=0, keepdims=True) + c_q)
            alpha.append(jnp.exp2(m_old - m_new))
            p_r = jnp.exp2(s_r - (m_new - c_q))
            l_ref[i, :, cols[r]] = alpha[r] * l_ref[i, :, cols[r]] + jnp.sum(p_r, axis=0, keepdims=True)
            m_ref[i, :, cols[r]] = m_new
            pr.append(p_r.astype(BF16))
        for r in subs:
            vt = vt_ref[j, :, :n_keys[r]]
            acc_ref[i, :, cols[r]] = (alpha[r] * acc_ref[i, :, cols[r]]
                                      + jnp.dot(vt, pr[r], preferred_element_type=F32))
        if causal:
            o_ref[0, i * blk:(i + 1) * blk, :] = (acc_ref[i] / l_ref[i]).T.astype(o_ref.dtype)

    pairs = [(i, j) for i in range(n_tiles) for j in range(i + 1)]
    logits(0, *pairs[0])
    for n, (i, j) in enumerate(pairs):
        if n + 1 < len(pairs):
            logits(n + 1, *pairs[n + 1])
        update(n, i, j)


def _fox_prompt(q, k, v, c, n_heads, blk=512):
    b, l, d = q.shape
    dh = d // n_heads
    blk = min(blk, l)
    nblk = l // blk
    ct = jnp.swapaxes(c, 1, 2).reshape(b, n_heads, nblk, blk)
    return pl.pallas_call(
        functools.partial(_fox_prompt_body, blk),
        grid=(b, n_heads),
        in_specs=[
            pl.BlockSpec((1, l, dh), lambda bi, h: (bi, 0, h)),
            pl.BlockSpec((1, l, dh), lambda bi, h: (bi, 0, h)),
            pl.BlockSpec((1, l, dh), lambda bi, h: (bi, 0, h)),
            pl.BlockSpec((1, l, n_heads), lambda bi, h: (bi, 0, 0)),
            pl.BlockSpec((1, 1, nblk, blk), lambda bi, h: (bi, h, 0, 0)),
        ],
        out_specs=pl.BlockSpec((1, l, dh), lambda bi, h: (bi, 0, h)),
        out_shape=jax.ShapeDtypeStruct((b, l, d), BF16),
        scratch_shapes=[pltpu.VMEM((l, LANES), F32), pltpu.VMEM((nblk, 1, blk), F32), pltpu.VMEM((nblk, 1, blk), F32),
                        pltpu.VMEM((nblk, dh, blk), F32), pltpu.VMEM((2, blk, blk), F32),
                        pltpu.VMEM((nblk, dh, blk), BF16)],
        compiler_params=_params(("parallel", "parallel")),
        name="fox_prompt",
    )(q, k, v, c, ct)


def _fox_decode_body(n_heads, dh, n_pp, pt_ref, q_ref, kn_ref, vn_ref, cn_ref, *refs):
    k_refs = refs[:n_pp]
    v_refs = refs[n_pp:2 * n_pp]
    lf_refs = refs[2 * n_pp:3 * n_pp]
    o_ref, m_ref, l_ref, acc_ref, carry_ref = refs[3 * n_pp:]
    step = pl.program_id(1)
    scale = dh ** -0.5
    q = q_ref[0].astype(BF16)

    @pl.when(step == 0)
    def _():
        kn = kn_ref[0].astype(BF16).astype(F32)
        m_ref[...] = jnp.sum(q.astype(F32) * kn, axis=-1, keepdims=True) * scale
        l_ref[...] = jnp.ones(l_ref.shape, F32)
        acc_ref[...] = vn_ref[0]
        carry_ref[...] = jnp.zeros(carry_ref.shape, F32)

    n_r, lanes = lf_refs[0].shape[1:]
    lane = lax.broadcasted_iota(jnp.int32, (n_r, lanes), 1)
    r_i = lax.broadcasted_iota(jnp.int32, (n_r, n_r), 0)
    c_i = lax.broadcasted_iota(jnp.int32, (n_r, n_r), 1)
    later_row = jnp.where(c_i > r_i, 1.0, 0.0).astype(F32)
    row_head = lax.broadcasted_iota(jnp.int32, (n_heads, lanes), 0)
    col_head = lax.broadcasted_iota(jnp.int32, (n_heads, lanes), 1) % n_heads
    own = row_head == col_head
    cn = cn_ref[0]

    for k_ref, v_ref, lf_ref in zip(k_refs, v_refs, lf_refs):
        lf = lf_ref[0]
        same_head_total = lf
        later_in_row = jnp.zeros_like(lf)
        for k in range(1, lanes // n_heads):
            same_head_total = same_head_total + pltpu.roll(lf, k * n_heads, 1)
            later_in_row = later_in_row + jnp.where(lane + k * n_heads < lanes,
                                                    pltpu.roll(lf, lanes - k * n_heads, 1), 0.0)
        later_rows = jnp.dot(later_row, same_head_total, precision=HIGHEST, preferred_element_type=F32)
        decay = later_in_row + later_rows + carry_ref[...]
        carry_ref[...] += jnp.sum(same_head_total, axis=0, keepdims=True)

        s = lax.dot_general(q, k_ref[0].astype(BF16), NT_DIMS, preferred_element_type=F32)
        s = jnp.concatenate(
            [jnp.where(own, s[:, r * lanes:(r + 1) * lanes] * scale + cn + decay[r:r + 1, :], -jnp.inf)
             for r in range(n_r)], axis=1)
        m_old = m_ref[...]
        m_new = jnp.maximum(m_old, jnp.max(s, axis=-1, keepdims=True))
        alpha = jnp.exp(m_old - m_new)
        pr = jnp.exp(s - m_new)
        l_ref[...] = alpha * l_ref[...] + jnp.sum(pr, axis=-1, keepdims=True)
        acc_ref[...] = alpha * acc_ref[...] + jnp.dot(pr.astype(BF16), v_ref[0].astype(BF16),
                                                      preferred_element_type=F32)
        m_ref[...] = m_new

    @pl.when(step == pl.num_programs(1) - 1)
    def _():
        o_ref[0] = acc_ref[...] / l_ref[...]


def _fox_decode(q, k_new, v_new, logf_new, cache_k, cache_v, cache_logf, page_table, n_pp=4):
    bd, n_heads, dh = q.shape
    n_pool, ps = cache_k.shape[:2]
    n_pages = page_table.shape[1]
    rows = ps * n_heads
    n_pp = math.gcd(n_pp, n_pages)
    assert LANES % n_heads == 0 and rows % LANES == 0
    tok = pl.BlockSpec((1, n_heads, dh), lambda b, s, pt: (b, 0, 0))

    def page_specs(block):
        return [pl.BlockSpec(block, functools.partial(
            lambda b, s, pt, i: (pt[b, n_pages - 1 - (s * n_pp + i)], 0, 0), i=i)) for i in range(n_pp)]

    grid_spec = pltpu.PrefetchScalarGridSpec(
        num_scalar_prefetch=1,
        grid=(bd, n_pages // n_pp),
        in_specs=[tok, tok, tok, pl.BlockSpec((1, n_heads, 1), lambda b, s, pt: (b, 0, 0)),
                  *page_specs((1, rows, dh)), *page_specs((1, rows, dh)),
                  *page_specs((1, rows // LANES, LANES))],
        out_specs=tok,
        scratch_shapes=[pltpu.VMEM((n_heads, 1), F32), pltpu.VMEM((n_heads, 1), F32),
                        pltpu.VMEM((n_heads, dh), F32), pltpu.VMEM((1, LANES), F32)],
    )
    return pl.pallas_call(
        functools.partial(_fox_decode_body, n_heads, dh, n_pp),
        grid_spec=grid_spec,
        out_shape=jax.ShapeDtypeStruct((bd, n_heads, dh), F32),
        compiler_params=_params(("parallel", "arbitrary")),
        name="fox_decode",
    )(page_table, q, k_new, v_new, logf_new.reshape(bd, n_heads, 1),
      *([cache_k.reshape(n_pool, rows, dh)] * n_pp), *([cache_v.reshape(n_pool, rows, dh)] * n_pp),
      *([cache_logf.reshape(n_pool, rows // LANES, LANES)] * n_pp))


class _Weights:
    def __init__(self, bn_first, **tensors):
        self.bn_first = bn_first
        self.t = tensors
        self.copies = {}

    def matmul(self, x, names, layer, offsets, n_cols, epi, out_dtypes, **kw):
        key = (tuple(names), layer)
        if key in self.copies:
            return _matmul(x, [(wb, 0) for wb in self.copies[key]], n_cols, epi, out_dtypes, **kw)
        stacked = [self.t[n] if self.t[n].ndim == 3 else self.t[n][None] for n in names]
        outs, self.copies[key] = _matmul_cast_first(x, list(zip(stacked, offsets)), layer, n_cols, epi, out_dtypes,
                                                    self.bn_first, **kw)
        return outs


def _ffn(h, w, layer, t):
    d_ff = w.t["ffn_w_down"].shape[1]
    (xn,) = _rmsnorm(h, [w.t["norm_ffn"][layer]], t["act"], t["bm_norm"])
    (a,) = w.matmul(xn, ["ffn_w_gate_up"] * 2, layer, (0, d_ff), d_ff, _epi_swiglu, (t["act"],),
                    bm=t["bm"], bn=t["bn_ff"])
    (h,) = w.matmul(a, ["ffn_w_down"], layer, (0,), h.shape[1], _epi_residual, (F32,), bm=t["bm"], bn=t["bn_down"],
                    bk=t["bk_down"], tiles=(h,))
    return h


def _trunk_front(x, w, s5, t, q_scale):
    m, dm = x.shape
    (xn,) = _rmsnorm(x, [w.t["norm_mix"][0]], t["act"], t["bm_norm"])
    (u,) = w.matmul(xn, ["ssm_w_in"], 0, (0,), dm, _epi_plain, (F32,), bm=t["bm"], bn=t["bn"])
    g_act, fin = s5(u)
    (h,) = w.matmul(g_act, ["ssm_w_glu"] * 2, 0, (0, dm), dm, _epi_glu_residual, (F32,), bm=t["bm"], bn=t["bn_glu"],
                    tiles=(x,))
    h = _ffn(h, w, 0, t)
    z, qn = _rmsnorm(h, [w.t["norm_kv"], w.t["norm_mix"][1]], t["act"], t["bm_norm"])
    k, kb, v, vb = w.matmul(z, ["kv_w_k", "kv_w_v"], 0, (0, 0), dm, _epi_kv, (F32, t["act"], F32, t["act"]),
                            bm=t["bm"], bn=t["bn_kv"])
    n_heads = w.t["kv_w_f"].shape[1]
    (logf,) = _matmul(z, [(w.t["kv_w_f"].astype(BF16), 0)], n_heads, _epi_log_sigmoid, (F32,), bm=t["bm"], bn=n_heads,
                      rows=(w.t["kv_b_f"].reshape(1, n_heads),))
    q_epi = _epi_plain if q_scale is None else functools.partial(_epi_scaled, q_scale)
    (q,) = w.matmul(qn, ["attn_w_q"], 0, (0,), dm, q_epi, (t["act"],), bm=t["bm"], bn=t["bn"])
    return h, q, k, v, kb, vb, logf, fin


def _trunk_back(h, att, w, t):
    (h,) = w.matmul(att, ["attn_w_o"], 0, (0,), h.shape[1], _epi_residual, (F32,), bm=t["bm"], bn=t["bn"], tiles=(h,))
    h = _ffn(h, w, 1, t)
    (y,) = _rmsnorm(h, [w.t["norm_final"]], F32, t["bm_norm"])
    return y


def _down_k_block(d_ff):
    units = d_ff // LANES
    for parts in range(2, units + 1):
        if units % parts == 0:
            return d_ff // parts
    return d_ff


def kernel(x_prompt, x_sample, state_ssm_re, state_ssm_im, cache_k, cache_v, cache_logf, page_table, norm_mix, norm_ffn, ssm_w_in, ssm_lambda_re, ssm_lambda_im, ssm_log_dt, ssm_b_re, ssm_b_im, ssm_c_re, ssm_c_im, ssm_d, ssm_w_glu, attn_w_q, attn_w_o, norm_kv, kv_w_k, kv_w_v, kv_w_f, kv_b_f, ffn_w_gate_up, ffn_w_down, norm_final):
    b, l, dm = x_prompt.shape
    bd, t_dec, _ = x_sample.shape
    assert t_dec == 1, "the sample group decodes one token per sequence"
    assert ssm_w_in.shape[0] == 1 and attn_w_q.shape[0] == 1, "one S5 layer followed by one FoX layer"
    g, p = ssm_lambda_re.shape[1:]
    n_heads = kv_w_f.shape[1]
    dh = dm // n_heads
    d_ff = ffn_w_down.shape[1]
    nc = l // SSM_CHUNK
    n_sq = max(1, (nc - 1).bit_length())

    w = _Weights(
        256,
        norm_mix=norm_mix, norm_ffn=norm_ffn, norm_kv=norm_kv, norm_final=norm_final, kv_b_f=kv_b_f, kv_w_f=kv_w_f,
        ssm_w_in=ssm_w_in, ssm_w_glu=ssm_w_glu, attn_w_q=attn_w_q, attn_w_o=attn_w_o, kv_w_k=kv_w_k, kv_w_v=kv_w_v,
        ffn_w_gate_up=ffn_w_gate_up, ffn_w_down=ffn_w_down,
    )
    bn_ff = math.gcd(d_ff, 1024)
    bk_down = _down_k_block(d_ff)
    tile_p = dict(act=BF16, bm=1024, bn=512, bn_kv=256, bn_glu=256, bn_ff=bn_ff, bn_down=512, bk_down=bk_down,
                  bm_norm=256)
    tile_s = dict(act=F32, bm=bd, bn=1024, bn_kv=512, bn_glu=512, bn_ff=bn_ff, bn_down=1024, bk_down=bk_down,
                  bm_norm=bd)

    tz, ws, wy, pws, ab, wc = _s5_prep(
        ssm_lambda_re[0], ssm_lambda_im[0], ssm_log_dt[0], ssm_b_re[0], ssm_b_im[0], ssm_c_re[0], ssm_c_im[0], n_sq)
    nt, _, sw = ws.shape

    def s5_prompt(u):
        x0 = jnp.zeros((b, nt, 1, sw), F32)
        return _s5_prompt(u, tz, ws, wy, pws, x0, ssm_d[0], b, p)

    xp = x_prompt.reshape(b * l, dm)
    h, q, k, v, kb, vb, logf, fin = _trunk_front(xp, w, s5_prompt, tile_p, dh ** -0.5 * LOG2E)
    cum = _cumsum_seq(logf.reshape(b, l, n_heads), LOG2E)
    att = _fox_prompt(q.reshape(b, l, dm), kb.reshape(b, l, dm), vb.reshape(b, l, dm), cum, n_heads)
    y_prompt = _trunk_back(h, att.reshape(b * l, dm), w, tile_p).reshape(b, l, dm)
    fin = fin.reshape(b, nt, 2, sw // 2)
    ssm_re_prompt = fin[:, :, 0].reshape(1, b, g, p)
    ssm_im_prompt = fin[:, :, 1].reshape(1, b, g, p)
    k_prompt = k.reshape(b, l, n_heads, dh)
    v_prompt = v.reshape(b, l, n_heads, dh)
    logf_prompt = logf.reshape(b, l, n_heads)

    def s5_sample(u):
        g_act, s_re, s_im = _s5_step(u, state_ssm_re[0], state_ssm_im[0], ws, ab, wc, ssm_d[0])
        return g_act, (s_re, s_im)

    xs = x_sample.reshape(bd, dm)
    hs, qs, ks, vs, _, _, logfs, (s_re, s_im) = _trunk_front(xs, w, s5_sample, tile_s, None)
    att_s = _fox_decode(qs.reshape(bd, n_heads, dh), ks.reshape(bd, n_heads, dh), vs.reshape(bd, n_heads, dh),
                        logfs, cache_k, cache_v, cache_logf, page_table)
    y_sample = _trunk_back(hs, att_s.reshape(bd, dm), w, tile_s).reshape(bd, 1, dm)
    ssm_re_sample = s_re.reshape(1, bd, g, p)
    ssm_im_sample = s_im.reshape(1, bd, g, p)
    k_sample = ks.reshape(bd, 1, n_heads, dh)
    v_sample = vs.reshape(bd, 1, n_heads, dh)
    logf_sample = logfs.reshape(bd, 1, n_heads)

    return (y_prompt, y_sample, ssm_re_prompt, ssm_im_prompt, k_prompt, v_prompt, logf_prompt,
            ssm_re_sample, ssm_im_sample, k_sample, v_sample, logf_sample)
```
